```python
import math
import jax, jax.numpy as jnp
from jax import lax
import numpy as np

D_MODEL = 1024
BATCH = 8
SEQ = 4096
DEPTH = 2

CTX_LEN = 256
GRID_W = 64
EPS = 1e-6

A_HEADS = 4
A_DK = 128
A_DV = 128
A_CHUNK = 16
B_HEADS = 4
B_DK = 128
B_DV = 128
B_CONV = 4
B_CHUNK = 64
C_WIDTH = D_MODEL
C_HEADS = 4
C_BLOCK = C_WIDTH // C_HEADS
C_CONV = 4
C_GATE_C = 8.0
FFN_DIM = 2816
FFN_CONV = 3

N_EVEN = (DEPTH + 1) // 2
N_ODD = DEPTH // 2

AB_SIZES = (A_HEADS * A_DK, A_HEADS * A_DK, A_HEADS * A_DK, A_HEADS * A_DV, A_HEADS * A_DV,
            2 * B_HEADS * B_DK + B_HEADS * B_DV, B_HEADS * B_DV, 2 * B_HEADS, 2 * B_HEADS)
AB_IN = 3 * A_HEADS * A_DK + 2 * A_HEADS * A_DV + 2 * B_HEADS * B_DK + 2 * B_HEADS * B_DV + 4 * B_HEADS
AB_OUT = A_HEADS * A_DV + B_HEADS * B_DV

kernel_name = 'hybrid_bidir_hgrn2_gdn_rglru_block'


def rms_norm(x, g):
    xf = x.astype(jnp.float32)
    y = xf * lax.rsqrt(jnp.mean(xf * xf, axis=-1, keepdims=True) + EPS)
    return (y * g.astype(jnp.float32)).astype(x.dtype)


def head_rms_norm(o, g):
    return o * lax.rsqrt(jnp.mean(o * o, axis=-1, keepdims=True) + EPS) * g.astype(jnp.float32)


def l2norm(t):
    return t * lax.rsqrt(jnp.sum(t * t, axis=-1, keepdims=True) + EPS)


def to_heads(t, n):
    b, s, _ = t.shape
    return t.reshape(b, s, n, -1).transpose(0, 2, 1, 3)


def from_heads(t):
    b, n, s, d = t.shape
    return t.transpose(0, 2, 1, 3).reshape(b, s, n * d)


def _split(p, sizes):
    out, start = [], 0
    for n in sizes:
        out.append(p[..., start:start + n])
        start += n
    return out


def dwconv1d(x, w, width):
    left = width // 2
    return lax.conv_general_dilated(x, w[:, None, :].astype(x.dtype), (1,), [(left, width - 1 - left)],
                                    dimension_numbers=('NWC', 'WIO', 'NWC'), feature_group_count=x.shape[-1])


def hgrn2_chunk_scan(q, k, v, logf, s0):
    b, nh, s, dk = q.shape
    dv = v.shape[-1]
    n, cl = s // A_CHUNK, A_CHUNK
    q, k, logf = [t.reshape(b, nh, n, cl, dk) for t in (q, k, logf)]
    v = v.reshape(b, nh, n, cl, dv)
    bc = jnp.cumsum(logf, axis=-2)
    q_in = q * jnp.exp(bc)
    k_in = k * jnp.exp(-bc)
    incl = jnp.tril(jnp.ones((cl, cl), bool))
    attn = jnp.where(incl, jnp.einsum('bhnid,bhnjd->bhnij', q_in, k_in), 0.0)
    intra = jnp.einsum('bhnij,bhnjv->bhniv', attn, v)
    kd = k * jnp.exp(bc[..., -1:, :] - bc)
    gl = jnp.exp(bc[..., -1, :])
    xs = [jnp.moveaxis(t, 2, 0) for t in (intra, q_in, kd, v, gl)]

    def step(st, xc):
        intra_c, q_c, kd_c, v_c, gl_c = xc
        o_c = intra_c + jnp.einsum('bhck,bhkv->bhcv', q_c, st)
        st = gl_c[..., None] * st + jnp.einsum('bhck,bhcv->bhkv', kd_c, v_c)
        return st, o_c

    st, o = lax.scan(step, s0, xs)
    return jnp.moveaxis(o, 0, 2).reshape(b, nh, s, dv), st


def gdn_chunk_scan(q, k, v, g, beta, s0):
    b, nh, s, dk = q.shape
    dv = v.shape[-1]
    n, cl = s // B_CHUNK, B_CHUNK
    q, k = [t.reshape(b, nh, n, cl, dk) for t in (q, k)]
    v = v.reshape(b, nh, n, cl, dv)
    g, beta = [t.reshape(b, nh, n, cl) for t in (g, beta)]
    gc = jnp.cumsum(g, axis=-1)
    incl = jnp.tril(jnp.ones((cl, cl), bool))
    strict = jnp.tril(jnp.ones((cl, cl), bool), -1)
    diff = gc[..., :, None] - gc[..., None, :]
    decay = jnp.where(incl, jnp.exp(jnp.where(incl, diff, 0.0)), 0.0)
    kb = k * beta[..., None]
    a_mat = jnp.where(strict, jnp.einsum('bhnid,bhnjd->bhnij', kb, k) * decay, 0.0)
    rhs = jnp.concatenate([v * beta[..., None], kb * jnp.exp(gc)[..., None]], axis=-1)
    sol = lax.linalg.triangular_solve(a_mat, rhs, left_side=True, lower=True, unit_diagonal=True)
    u, w = sol[..., :dv], sol[..., dv:]
    attn = jnp.einsum('bhnid,bhnjd->bhnij', q, k) * decay
    qg = q * jnp.exp(gc)[..., None]
    kd = k * jnp.exp(gc[..., -1:] - gc)[..., None]
    gl = jnp.exp(gc[..., -1])
    xs = [jnp.moveaxis(t, 2, 0) for t in (u, w, attn, qg, kd, gl)]

    def step(st, xc):
        u_c, w_c, attn_c, qg_c, kd_c, gl_c = xc
        v_new = u_c - jnp.einsum('bhck,bhkv->bhcv', w_c, st)
        o_c = jnp.einsum('bhck,bhkv->bhcv', qg_c, st) + jnp.einsum('bhcj,bhjv->bhcv', attn_c, v_new)
        st = gl_c[..., None, None] * st + jnp.einsum('bhck,bhcv->bhkv', kd_c, v_new)
        return st, o_c

    st, o = lax.scan(step, s0, xs)
    return jnp.moveaxis(o, 0, 2).reshape(b, nh, s, dv), st


def linear_scan(a, u, h0):
    def combine(l, r):
        return l[0] * r[0], r[0] * l[1] + r[1]
    a_cum, u_cum = lax.associative_scan(combine, (a, u), axis=1)
    h = a_cum * h0[:, None, :] + u_cum
    return h, h[:, -1]


def bidirectional(scan_fn, ctx_args, lat_args, s0, time_axis):
    ys_x, ys_c = [], []
    for d in range(2):
        rev = (lambda t: jnp.flip(t, axis=time_axis)) if d == 1 else (lambda t: t)
        oc, sc = scan_fn(*[rev(t) for t in ctx_args[d]], s0)
        ox, _ = scan_fn(*[rev(t) for t in lat_args[d]], sc)
        ys_c.append(rev(oc))
        ys_x.append(rev(ox))
    return ys_x[0] + ys_x[1], ys_c[0] + ys_c[1]


def _ab_features(h, w_in, lb, conv_w, a_log, dt_bias):
    f32 = jnp.float32
    bsz, s, _ = h.shape
    aq, af_f, af_b, ai, ag, bqkv, bz, ba, bb = _split(h @ w_in, AB_SIZES)
    ka, lfa = [], []
    for f_pre, lb_d in ((af_f, lb[0]), (af_b, lb[1])):
        f_pre = f_pre.astype(f32)
        ka.append(to_heads((1.0 - lb_d) * jax.nn.sigmoid(-f_pre), A_HEADS))
        lfa.append(to_heads(jnp.log(lb_d + (1.0 - lb_d) * jax.nn.sigmoid(f_pre)), A_HEADS))
    qkv = jax.nn.silu(dwconv1d(bqkv, conv_w, B_CONV)).astype(f32)
    bq, bk, bv = _split(qkv, (B_HEADS * B_DK, B_HEADS * B_DK, B_HEADS * B_DV))
    ba = ba.astype(f32).reshape(bsz, s, 2, B_HEADS)
    bb = bb.astype(f32).reshape(bsz, s, 2, B_HEADS)
    g = (-jnp.exp(a_log.astype(f32)) * jax.nn.softplus(ba + dt_bias.astype(f32))).transpose(2, 0, 3, 1)
    beta = jax.nn.sigmoid(bb).transpose(2, 0, 3, 1)
    return dict(
        qa=to_heads(jax.nn.silu(aq.astype(f32)), A_HEADS) * A_DK ** -0.5,
        ka=ka, va=to_heads(ai.astype(f32), A_HEADS), lfa=lfa, ga=ag,
        qb=l2norm(to_heads(bq, B_HEADS)) * B_DK ** -0.5,
        kb=l2norm(to_heads(bk, B_HEADS)), vb=to_heads(bv, B_HEADS),
        gb=g, betab=beta, zb=bz)


def mixer_ab(h, hc, w_in, lb, hgrn_g, conv_w, a_log, dt_bias, gdn_g, w_out, need_ctx):
    f32 = jnp.float32
    fx = _ab_features(h, w_in, lb, conv_w, a_log, dt_bias)
    fc = _ab_features(hc, w_in, lb, conv_w, a_log, dt_bias)
    bsz = h.shape[0]

    def a_args(f):
        return [(f['qa'], f['ka'][d], f['va'], f['lfa'][d]) for d in range(2)]

    def b_args(f):
        return [(f['qb'], f['kb'], f['vb'], f['gb'][d], f['betab'][d]) for d in range(2)]

    oa_x, oa_c = bidirectional(hgrn2_chunk_scan, a_args(fc), a_args(fx),
                               jnp.zeros((bsz, A_HEADS, A_DK, A_DV), f32), 2)
    ob_x, ob_c = bidirectional(gdn_chunk_scan, b_args(fc), b_args(fx),
                               jnp.zeros((bsz, B_HEADS, B_DK, B_DV), f32), 2)

    def merge(oa, ob, f):
        ya = from_heads(head_rms_norm(oa, hgrn_g)) * jax.nn.silu(f['ga'].astype(f32))
        yb = from_heads(head_rms_norm(ob, gdn_g)) * jax.nn.silu(f['zb'].astype(f32))
        return jnp.concatenate([ya, yb], axis=-1).astype(h.dtype) @ w_out

    y = merge(oa_x, ob_x, fx)
    yc = merge(oa_c, ob_c, fc) if need_ctx else None
    return y, yc


def _c_features(h, w_in, conv_w, conv_b, gate_w, gate_b, lam):
    f32 = jnp.float32
    gate, xb = jnp.split(h @ w_in, 2, axis=-1)
    xb = (dwconv1d(xb, conv_w, C_CONV) + conv_b).astype(f32)
    bsz, s, _ = xb.shape
    gl = jnp.einsum('bshi,dhio->dbsho', xb.reshape(bsz, s, C_HEADS, C_BLOCK), gate_w.astype(f32))
    gl = jax.nn.sigmoid(gl + gate_b.astype(f32)[:, None, None])
    r, i = jnp.split(gl, 2, axis=-1)
    r = r.reshape(2, bsz, s, C_WIDTH)
    i = i.reshape(2, bsz, s, C_WIDTH)
    log_a = -C_GATE_C * r * jax.nn.softplus(-lam.astype(f32))[:, None, None, :]
    a = jnp.exp(log_a)
    u = jnp.sqrt(-jnp.expm1(2.0 * log_a)) * (i * xb[None])
    return gate, a, u


def mixer_c(h, hc, w_in, conv_w, conv_b, gate_w, gate_b, lam, w_out, need_ctx):
    f32 = jnp.float32
    gx, ax, ux = _c_features(h, w_in, conv_w, conv_b, gate_w, gate_b, lam)
    gc, ac, uc = _c_features(hc, w_in, conv_w, conv_b, gate_w, gate_b, lam)
    s0 = jnp.zeros((h.shape[0], C_WIDTH), f32)
    hx, hcx = bidirectional(linear_scan, [(ac[d], uc[d]) for d in range(2)],
                            [(ax[d], ux[d]) for d in range(2)], s0, 1)

    def out(g, r):
        return (jax.nn.gelu(g.astype(f32)) * r).astype(h.dtype) @ w_out

    return out(gx, hx), (out(gc, hcx) if need_ctx else None)


def conv_ffn(h, w_up, conv_w, conv_b, w_down, rows):
    a, v = jnp.split(h @ w_up, 2, axis=-1)
    if rows is None:
        a = dwconv1d(a, conv_w[FFN_CONV // 2], FFN_CONV)
    else:
        b, s, f = a.shape
        pad = FFN_CONV // 2
        a = lax.conv_general_dilated(a.reshape(b, rows, GRID_W, f), conv_w[:, :, None, :].astype(a.dtype), (1, 1),
                                     [(pad, pad), (pad, pad)], dimension_numbers=('NHWC', 'HWIO', 'NHWC'),
                                     feature_group_count=f).reshape(b, s, f)
    return (jax.nn.silu(a + conv_b) * v) @ w_down


def setup_inputs(seed: int = 0) -> dict:
    key = jax.random.key(seed)
    ks = jax.random.split(key, 28)
    cnt = [0]
    f32 = jnp.float32

    def nxt():
        k = ks[cnt[0]]
        cnt[0] += 1
        return k

    def nrm(shape, scale):
        return jax.random.normal(nxt(), shape, f32) * scale

    def unif(shape, lo, hi):
        return jax.random.uniform(nxt(), shape, f32, lo, hi)

    D = D_MODEL
    x = nrm((BATCH, SEQ, D), 1.0)
    c = nrm((BATCH, D), 1.0)
    ctx = nrm((BATCH, CTX_LEN, D), 1.0)
    c_ctx = nrm((D,), 1.0)
    mod_w = nrm((DEPTH, D, 6 * D), 0.5 * D ** -0.5)
    mod_b = nrm((DEPTH, 6 * D), 0.02)
    norm_mix_g = 1.0 + nrm((DEPTH, D), 0.02)
    norm_ffn_g = 1.0 + nrm((DEPTH, D), 0.02)
    final_norm_g = 1.0 + nrm((D,), 0.02)
    ab_w_in = nrm((N_EVEN, D, AB_IN), D ** -0.5)
    hgrn_lb = nrm((2, DEPTH + 1, A_HEADS * A_DK), 0.1)
    hgrn_norm_g = 1.0 + nrm((N_EVEN, A_DV), 0.02)
    gdn_conv_w = nrm((N_EVEN, B_CONV, 2 * B_HEADS * B_DK + B_HEADS * B_DV), B_CONV ** -0.5)
    gdn_a_log = jnp.log(unif((N_EVEN, 2, B_HEADS), 1.0, 16.0))
    dt = jnp.exp(unif((N_EVEN, 2, B_HEADS), math.log(1e-3), math.log(1e-1)))
    gdn_dt_bias = dt + jnp.log(-jnp.expm1(-dt))
    gdn_norm_g = 1.0 + nrm((N_EVEN, B_DV), 0.02)
    ab_w_out = nrm((N_EVEN, AB_OUT, D), AB_OUT ** -0.5)
    c_w_in = nrm((N_ODD, D, 2 * C_WIDTH), D ** -0.5)
    c_conv_w = nrm((N_ODD, C_CONV, C_WIDTH), C_CONV ** -0.5)
    c_conv_b = nrm((N_ODD, C_WIDTH), 0.02)
    c_gate_w = nrm((N_ODD, 2, C_HEADS, C_BLOCK, 2 * C_BLOCK), C_BLOCK ** -0.5)
    c_gate_b = nrm((N_ODD, 2, C_HEADS, 2 * C_BLOCK), 0.02)
    u = unif((N_ODD, 2, C_WIDTH), 0.9, 0.999)
    c_lambda = jnp.log(u) - jnp.log1p(-u)
    c_w_out = nrm((N_ODD, C_WIDTH, D), C_WIDTH ** -0.5)
    ffn_w_up = nrm((DEPTH, D, 2 * FFN_DIM), D ** -0.5)
    ffn_conv_w = nrm((DEPTH, FFN_CONV, FFN_CONV, FFN_DIM), 1.0 / FFN_CONV)
    ffn_conv_b = nrm((DEPTH, FFN_DIM), 0.02)
    ffn_w_down = nrm((DEPTH, FFN_DIM, D), FFN_DIM ** -0.5)
    return {'x': x, 'c': c, 'ctx': ctx, 'c_ctx': c_ctx, 'mod_w': mod_w, 'mod_b': mod_b,
            'norm_mix_g': norm_mix_g, 'norm_ffn_g': norm_ffn_g, 'final_norm_g': final_norm_g,
            'ab_w_in': ab_w_in, 'hgrn_lb': hgrn_lb, 'hgrn_norm_g': hgrn_norm_g, 'gdn_conv_w': gdn_conv_w,
            'gdn_a_log': gdn_a_log, 'gdn_dt_bias': gdn_dt_bias, 'gdn_norm_g': gdn_norm_g, 'ab_w_out': ab_w_out,
            'c_w_in': c_w_in, 'c_conv_w': c_conv_w, 'c_conv_b': c_conv_b, 'c_gate_w': c_gate_w,
            'c_gate_b': c_gate_b, 'c_lambda': c_lambda, 'c_w_out': c_w_out,
            'ffn_w_up': ffn_w_up, 'ffn_conv_w': ffn_conv_w, 'ffn_conv_b': ffn_conv_b, 'ffn_w_down': ffn_w_down}


def reference(x, c, ctx, c_ctx, mod_w, mod_b, norm_mix_g, norm_ffn_g, final_norm_g,
              ab_w_in, hgrn_lb, hgrn_norm_g, gdn_conv_w, gdn_a_log, gdn_dt_bias, gdn_norm_g, ab_w_out,
              c_w_in, c_conv_w, c_conv_b, c_gate_w, c_gate_b, c_lambda, c_w_out,
              ffn_w_up, ffn_conv_w, ffn_conv_b, ffn_w_down):
    rows = x.shape[1] // GRID_W
    lb_all = jnp.cumsum(jax.nn.softmax(hgrn_lb.astype(jnp.float32), axis=1), axis=1)
    xc = ctx
    for l in range(DEPTH):
        last = l == DEPTH - 1
        mod = jax.nn.silu(c) @ mod_w[l] + mod_b[l]
        modc = jax.nn.silu(c_ctx) @ mod_w[l] + mod_b[l]
        sh1, sc1, g1, sh2, sc2, g2 = jnp.split(mod[:, None, :], 6, axis=-1)
        sh1c, sc1c, g1c, sh2c, sc2c, g2c = jnp.split(modc, 6, axis=-1)
        h = rms_norm(x, norm_mix_g[l]) * (1.0 + sc1) + sh1
        hc = rms_norm(xc, norm_mix_g[l]) * (1.0 + sc1c) + sh1c
        if l % 2 == 0:
            e = l // 2
            y, yc = mixer_ab(h, hc, ab_w_in[e], lb_all[:, l], hgrn_norm_g[e], gdn_conv_w[e], gdn_a_log[e],
                             gdn_dt_bias[e], gdn_norm_g[e], ab_w_out[e], not last)
        else:
            o = l // 2
            y, yc = mixer_c(h, hc, c_w_in[o], c_conv_w[o], c_conv_b[o], c_gate_w[o], c_gate_b[o],
                            c_lambda[o], c_w_out[o], not last)
        x = x + g1 * y
        h = rms_norm(x, norm_ffn_g[l]) * (1.0 + sc2) + sh2
        x = x + g2 * conv_ffn(h, ffn_w_up[l], ffn_conv_w[l], ffn_conv_b[l], ffn_w_down[l], rows)
        if not last:
            xc = xc + g1c * yc
            hc = rms_norm(xc, norm_ffn_g[l]) * (1.0 + sc2c) + sh2c
            xc = xc + g2c * conv_ffn(hc, ffn_w_up[l], ffn_conv_w[l], ffn_conv_b[l], ffn_w_down[l], None)
    return rms_norm(x, final_norm_g)
```

```python
import functools
import math

import numpy as np
import jax
import jax.numpy as jnp
from jax import lax
from jax.experimental import pallas as pl
from jax.experimental.pallas import tpu as pltpu

F32 = jnp.float32
BF16 = jnp.bfloat16

D_MODEL = 1024
DEPTH = 2
GRID_W = 64
EPS = 1e-6
A_HEADS = 4
A_DK = 128
A_DV = 128
B_HEADS = 4
B_DK = 128
B_DV = 128
B_CONV = 4
C_WIDTH = D_MODEL
C_HEADS = 4
C_BLOCK = C_WIDTH // C_HEADS
C_CONV = 4
C_GATE_C = 8.0
FFN_DIM = 2816
FFN_CONV = 3

HD = A_HEADS * A_DK
QKV = 2 * B_HEADS * B_DK + B_HEADS * B_DV

LANE = 128
SUBLANE = 8
VMEM_LIMIT = 56 * 1024 * 1024

TB = 256
SUB = 16
LEVELS = (16, 32, 64, 128)
FT = 256
FFN_TM = 512
CONV_HALO = 8


def _cp(sem):
    return pltpu.CompilerParams(dimension_semantics=sem, vmem_limit_bytes=VMEM_LIMIT)


def _dot(a, b):
    return jnp.dot(a, b, preferred_element_type=F32)


def _dot_nt(a, b):
    return lax.dot_general(a, b, (((1,), (1,)), ((), ())), preferred_element_type=F32)


def _dot_tn(a, b):
    return lax.dot_general(a, b, (((0,), (0,)), ((), ())), preferred_element_type=F32)


def _sigmoid(x):
    return 1.0 / (1.0 + jnp.exp(-x))


def _silu(x):
    return x * _sigmoid(x)


def _softplus(x):
    return jnp.maximum(x, 0.0) + jnp.log1p(jnp.exp(-jnp.abs(x)))


def _gelu_tanh(x):
    c = math.sqrt(2.0 / math.pi)
    return 0.5 * x * (1.0 + jnp.tanh(c * (x + 0.044715 * (x * x * x))))


def _hi_lo(x):
    hi = x.astype(BF16)
    lo = (x - hi.astype(F32)).astype(BF16)
    return hi, lo


def _norm_mod(x, g, sc, sh):
    ms = jnp.mean(x * x, axis=-1, keepdims=True)
    return (x * lax.rsqrt(ms + EPS)) * g * (1.0 + sc) + sh


def _const_spec(shape):
    nd = len(shape)
    return pl.BlockSpec(shape, lambda *_: (0,) * nd, pipeline_mode=pl.Buffered(1))


MOD_TN = 1024


def _mod_kernel(c_ref, w_ref, b_ref, o_ref):
    s = _silu(c_ref[...])
    w = w_ref[0]
    s_hi, s_lo = _hi_lo(s)
    w_hi, w_lo = _hi_lo(w)
    acc = _dot(s_hi, w_hi) + _dot(s_hi, w_lo) + _dot(s_lo, w_hi)
    o_ref[0] = acc + b_ref[0]


def _modulation(cc, mod_w, mod_b):
    rows = cc.shape[0]
    n = mod_w.shape[-1]
    return pl.pallas_call(
        _mod_kernel,
        grid=(DEPTH, n // MOD_TN),
        in_specs=[pl.BlockSpec((rows, D_MODEL), lambda l, j: (0, 0)),
                  pl.BlockSpec((1, D_MODEL, MOD_TN), lambda l, j: (l, 0, j)),
                  pl.BlockSpec((1, 1, MOD_TN), lambda l, j: (l, 0, j))],
        out_specs=pl.BlockSpec((1, rows, MOD_TN), lambda l, j: (l, 0, j)),
        out_shape=jax.ShapeDtypeStruct((DEPTH, rows, n), F32),
        compiler_params=_cp(("arbitrary", "arbitrary")),
    )(cc, mod_w, mod_b.reshape(DEPTH, 1, n))


def _halo_specs(t_len, width):
    per = TB // CONV_HALO
    last = t_len // CONV_HALO - 1
    main = pl.BlockSpec((1, TB, width), lambda b, i: (b, i, 0))
    prev = pl.BlockSpec((1, CONV_HALO, width), lambda b, i: (b, jnp.maximum(i * per - 1, 0), 0))
    nxt = pl.BlockSpec((1, CONV_HALO, width), lambda b, i: (b, jnp.minimum((i + 1) * per, last), 0))
    return main, prev, nxt


def _mod_spec(per_batch):
    if per_batch:
        return pl.BlockSpec((1, SUBLANE, D_MODEL), lambda b, *_: (b, 0, 0))
    return pl.BlockSpec((1, SUBLANE, D_MODEL), lambda b, *_: (0, 0, 0))


def _conv_tokens(ext_ref, w_ref, width):
    left = width // 2
    acc = None
    for j in range(width):
        term = w_ref[j:j + 1, :] * ext_ref[pl.ds(CONV_HALO + j - left, TB), :]
        acc = term if acc is None else acc + term
    return acc


def _ab_in_kernel(layer, x_ref, xp_ref, xn_ref, mod_ref, ng_ref, wa_ref, wqkv_ref, wz_ref, wab_ref, wabt_ref,
                  lbp_ref, convw_ref, coef_ref, coeft_ref,
                  qa_ref, ka_ref, lfa_ref, va_ref, ga_ref, qb_ref, kb_ref, vb_ref, zb_ref, gbc_ref, gbr_ref,
                  ext_ref):
    i = pl.program_id(1)
    nt = pl.num_programs(1)
    sh = mod_ref[0, 0:1, :]
    sc = mod_ref[0, 1:2, :]
    g = ng_ref[...]
    hb = _norm_mod(x_ref[0], g, sc, sh).astype(BF16)

    pa = _dot(hb, wa_ref[...])
    qa_ref[0] = _silu(pa[:, 0:HD]) * (A_DK ** -0.5)
    va_ref[0] = pa[:, 3 * HD:4 * HD]
    ga_ref[0] = _silu(pa[:, 4 * HD:5 * HD])
    for d in range(2):
        rws = [lbp_ref[d * (DEPTH + 1) + r:d * (DEPTH + 1) + r + 1, :] for r in range(DEPTH + 1)]
        mx = functools.reduce(jnp.maximum, rws)
        es = [jnp.exp(r - mx) for r in rws]
        lb_d = functools.reduce(jnp.add, es[:layer + 1]) / functools.reduce(jnp.add, es)
        f_pre = pa[:, (1 + d) * HD:(2 + d) * HD]
        ka_ref[d, 0] = (1.0 - lb_d) * _sigmoid(-f_pre)
        lfa_ref[d, 0] = jnp.log(lb_d + (1.0 - lb_d) * _sigmoid(f_pre))

    hp = _norm_mod(xp_ref[0], g, sc, sh).astype(BF16)
    hn = _norm_mod(xn_ref[0], g, sc, sh).astype(BF16)
    keep_p = (i > 0).astype(F32)
    keep_n = (i < nt - 1).astype(F32)
    ext_ref[0:CONV_HALO, :] = _dot(hp, wqkv_ref[...]) * keep_p
    ext_ref[CONV_HALO:CONV_HALO + TB, :] = _dot(hb, wqkv_ref[...])
    ext_ref[CONV_HALO + TB:, :] = _dot(hn, wqkv_ref[...]) * keep_n
    qkv = _silu(_conv_tokens(ext_ref, convw_ref, B_CONV))
    for hh in range(B_HEADS):
        for part, (ref, scale) in enumerate(((qb_ref, B_DK ** -0.5), (kb_ref, 1.0))):
            lo = part * HD + hh * B_DK
            t = qkv[:, lo:lo + B_DK]
            ss = jnp.sum(t * t, axis=-1, keepdims=True)
            ref[0, :, hh * B_DK:(hh + 1) * B_DK] = t * (lax.rsqrt(ss + EPS) * scale)
    vb_ref[0] = qkv[:, 2 * HD:3 * HD]
    zb_ref[0] = _silu(_dot(hb, wz_ref[...]))

    ab = _dot(hb, wab_ref[...])
    lane = lax.broadcasted_iota(jnp.int32, ab.shape, 1)
    is_g = (lane % SUBLANE) < B_HEADS
    a_coef = coef_ref[0:1, :]
    dt_b = coef_ref[1:2, :]
    act = jnp.where(is_g, -jnp.exp(a_coef) * _softplus(ab + dt_b), _sigmoid(ab))
    abt = _dot_nt(wabt_ref[...], hb)
    row = lax.broadcasted_iota(jnp.int32, abt.shape, 0)
    is_g_t = (row % SUBLANE) < B_HEADS
    act_t = jnp.where(is_g_t, -jnp.exp(coeft_ref[:, 0:1]) * _softplus(abt + coeft_ref[:, 1:2]), _sigmoid(abt))
    for d in range(2):
        col = lax.broadcasted_iota(jnp.int32, (TB, LANE), 1)
        shifted = act if d == 0 else pltpu.roll(act, LANE - SUBLANE, axis=1)
        gbc_ref[0, d] = jnp.where(col < SUBLANE, shifted, 0.0)
        gbr_ref[0, d] = act_t[d * SUBLANE:(d + 1) * SUBLANE, :]


def _ab_in(layer, x, mod, per_batch, ng, w, hgrn_lb, conv_w, a_log, dt_bias):
    bsz, t_len, _ = x.shape
    nt = t_len // TB
    main, prev, nxt = _halo_specs(t_len, D_MODEL)
    tok = lambda width: pl.BlockSpec((1, TB, width), lambda b, i: (b, i, 0))
    tok2 = lambda width: pl.BlockSpec((2, 1, TB, width), lambda b, i: (0, b, i, 0))
    sds = lambda *s: jax.ShapeDtypeStruct(s, F32)
    out_shape = (sds(bsz, t_len, HD), sds(2, bsz, t_len, HD), sds(2, bsz, t_len, HD), sds(bsz, t_len, HD),
                 sds(bsz, t_len, HD), sds(bsz, t_len, HD), sds(bsz, t_len, HD), sds(bsz, t_len, HD),
                 sds(bsz, t_len, HD), sds(bsz, 2, t_len, LANE), sds(bsz, 2, SUBLANE, t_len))
    out_specs = (tok(HD), tok2(HD), tok2(HD), tok(HD), tok(HD), tok(HD), tok(HD), tok(HD), tok(HD),
                 pl.BlockSpec((1, 2, TB, LANE), lambda b, i: (b, 0, i, 0)),
                 pl.BlockSpec((1, 2, SUBLANE, TB), lambda b, i: (b, 0, 0, i)))
    w_al, w_be = w['alpha'], w['beta']
    cols = [w_al[:, 0:4], w_be[:, 0:4], w_al[:, 4:8], w_be[:, 4:8]]
    wab = jnp.concatenate(cols, axis=1)
    wab_p = jnp.pad(wab, ((0, 0), (0, LANE - 2 * SUBLANE))).astype(BF16)
    wabt = wab.T.astype(BF16)
    zero4 = jnp.zeros((B_HEADS,), F32)
    a_flat = jnp.concatenate([a_log[0], zero4, a_log[1], zero4])
    d_flat = jnp.concatenate([dt_bias[0], zero4, dt_bias[1], zero4])
    coef = jnp.pad(jnp.stack([a_flat, d_flat]), ((0, 0), (0, LANE - 2 * SUBLANE)))
    coeft = jnp.stack([a_flat, d_flat], axis=1)
    hgrn_lb = hgrn_lb.reshape(2 * (DEPTH + 1), HD)
    args = (x, x, x, mod, ng.reshape(1, D_MODEL), w['a'], w['qkv'], w['z'], wab_p, wabt, hgrn_lb, conv_w, coef,
            coeft)
    in_specs = [main, prev, nxt, _mod_spec(per_batch), _const_spec((1, D_MODEL)), _const_spec(w['a'].shape),
                _const_spec(w['qkv'].shape), _const_spec(w['z'].shape), _const_spec(wab_p.shape),
                _const_spec(wabt.shape), _const_spec(hgrn_lb.shape), _const_spec(conv_w.shape),
                _const_spec(coef.shape), _const_spec(coeft.shape)]
    return pl.pallas_call(
        functools.partial(_ab_in_kernel, layer),
        grid=(bsz, nt),
        in_specs=in_specs,
        out_specs=out_specs,
        out_shape=out_shape,
        scratch_shapes=[pltpu.VMEM((TB + 2 * CONV_HALO, QKV), F32)],
        compiler_params=_cp(("arbitrary", "arbitrary")),
    )(*args)


def _both_dirs(stack):
    return np.stack([stack, stack[:, ::-1, ::-1]])


@functools.lru_cache(maxsize=None)
def _hgrn_consts():
    t = np.arange(TB)
    i, j = t[:, None], t[None, :]
    sums = [(i // SUB == j // SUB) & (j <= i)]
    masks = [(i // SUB == j // SUB) & (j <= i)]
    for m in LEVELS:
        seg = 2 * m
        same = (i // seg) == (j // seg)
        mid = (i // seg) * seg + m - 1
        second = (i % seg) >= m
        sums.append(same & np.where(second, (j > mid) & (j <= i), (j > i) & (j <= mid)))
        masks.append(same & second & ((j % seg) < m))
    sums.append(j <= i)
    sums.append(j > i)
    sums = _both_dirs(np.stack(sums).astype(np.float32))
    masks = _both_dirs(np.stack(masks).astype(np.float32))
    n = sums.shape[1]
    return jnp.asarray(sums.reshape(2, n * TB, TB), BF16), jnp.asarray(masks, F32)


@functools.lru_cache(maxsize=None)
def _gdn_consts():
    t = np.arange(TB)
    i, j = t[:, None], t[None, :]
    tri = _both_dirs(np.stack([j <= i, j < i]).astype(np.float32))
    blocks = [(i // SUB) == (j // SUB)]
    for m in LEVELS:
        seg = 2 * m
        blocks.append(((i // seg) == (j // seg)) & ((i // m) != (j // m)))
    blocks = np.stack(blocks).astype(np.float32)
    return jnp.asarray(tri[:, 0], BF16), jnp.asarray(tri, F32), jnp.asarray(blocks, F32)


def _time_block(d, i, nt):
    return i + d * (nt - 1 - 2 * i)


def _hgrn_scan_kernel(q_ref, k_ref, lf_ref, v_ref, sums_ref, masks_ref, s0_ref, o_ref, sfin_ref, st_ref):
    i = pl.program_id(2)
    nt = pl.num_programs(2)

    @pl.when(i == 0)
    def _():
        st_ref[...] = s0_ref[0, 0]

    n_lv = len(LEVELS)
    for hh in range(A_HEADS):
        sl = slice(hh * A_DK, (hh + 1) * A_DK)
        q = q_ref[0, :, sl]
        k = k_ref[0, 0, :, sl]
        lf = lf_ref[0, 0, :, sl]
        v = v_ref[0, :, sl]
        lf_hi, lf_lo = _hi_lo(lf)
        xx = _dot(sums_ref[0], jnp.concatenate([lf_hi, lf_lo], axis=1))
        xx = xx[:, :A_DK] + xx[:, A_DK:]
        x0 = xx[0:TB]
        raw = _dot_nt((q * jnp.exp(x0)).astype(BF16), (k * jnp.exp(-x0)).astype(BF16))
        attn = jnp.where(masks_ref[0, 0] > 0.0, raw, 0.0)
        for lv in range(1, n_lv + 1):
            e = jnp.exp(xx[lv * TB:(lv + 1) * TB])
            attn = attn + masks_ref[0, lv] * _dot_nt((q * e).astype(BF16), (k * e).astype(BF16))
        e_in = jnp.exp(xx[(n_lv + 1) * TB:(n_lv + 2) * TB])
        e_out = jnp.exp(xx[(n_lv + 2) * TB:(n_lv + 3) * TB])
        st = st_ref[hh]
        o = _dot(attn.astype(BF16), v.astype(BF16)) + _dot_nt((q * e_in).astype(BF16), st.astype(BF16))
        o_ref[0, 0, :, sl] = o
        gl = jnp.exp(jnp.sum(lf, axis=0, keepdims=True))
        st_ref[hh] = st * gl + _dot_tn(v.astype(BF16), (k * e_out).astype(BF16))

    @pl.when(i == nt - 1)
    def _():
        sfin_ref[0, 0] = st_ref[...]


def _hgrn_scan(q, k2, lf2, v, s0):
    bsz, t_len, _ = q.shape
    nt = t_len // TB
    sums, masks = _hgrn_consts()
    tok = pl.BlockSpec((1, TB, HD), lambda b, d, i: (b, _time_block(d, i, nt), 0))
    tok2 = pl.BlockSpec((1, 1, TB, HD), lambda b, d, i: (d, b, _time_block(d, i, nt), 0))
    st_spec = pl.BlockSpec((1, 1, A_HEADS, A_DV, A_DK), lambda b, d, i: (b, d, 0, 0, 0))
    return pl.pallas_call(
        _hgrn_scan_kernel,
        grid=(bsz, 2, nt),
        in_specs=[tok, tok2, tok2, tok,
                  pl.BlockSpec((1,) + sums.shape[1:], lambda b, d, i: (d, 0, 0)),
                  pl.BlockSpec((1,) + masks.shape[1:], lambda b, d, i: (d, 0, 0, 0)),
                  st_spec],
        out_specs=(tok2, st_spec),
        out_shape=(jax.ShapeDtypeStruct((2, bsz, t_len, HD), F32),
                   jax.ShapeDtypeStruct((bsz, 2, A_HEADS, A_DV, A_DK), F32)),
        scratch_shapes=[pltpu.VMEM((A_HEADS, A_DV, A_DK), F32)],
        compiler_params=_cp(("arbitrary", "arbitrary", "arbitrary")),
    )(q, k2, lf2, v, sums, masks, s0)


def _unit_tri_inverse_minus_identity(a, blocks_ref):
    b16 = lambda x: x.astype(BF16)
    ad = a * blocks_ref[0]
    e = -ad
    p = ad
    width = 1
    while 2 * width < SUB:
        p = _dot(b16(p), b16(p))
        e = e + p + _dot(b16(e), b16(p))
        width *= 2
    for lv in range(len(LEVELS)):
        bm = a * blocks_ref[lv + 1]
        y = bm + _dot(b16(e), b16(bm))
        e = e - (y + _dot(b16(y), b16(e)))
    return e


def _gdn_scan_kernel(q_ref, k_ref, v_ref, gbc_ref, gbr_ref, linc_ref, tri_ref, blocks_ref, s0_ref,
                     o_ref, sfin_ref, st_ref):
    i = pl.program_id(2)
    nt = pl.num_programs(2)

    @pl.when(i == 0)
    def _():
        st_ref[...] = s0_ref[0, 0]

    gcol = gbc_ref[0, 0]
    grow = gbr_ref[0, 0]
    linc = linc_ref[0]
    c_hi, c_lo = _hi_lo(gcol)
    gc_col = _dot(linc, jnp.concatenate([c_hi, c_lo], axis=1))
    gc_col = gc_col[:, :LANE] + gc_col[:, LANE:]
    r_hi, r_lo = _hi_lo(grow)
    gc_row = _dot_nt(jnp.concatenate([r_hi, r_lo], axis=0), linc)
    gc_row = gc_row[:SUBLANE] + gc_row[SUBLANE:]
    g_tot = jnp.sum(gcol, axis=0, keepdims=True)
    incl = tri_ref[0, 0] > 0.0
    strict = tri_ref[0, 1]

    for hh in range(B_HEADS):
        sl = slice(hh * B_DK, (hh + 1) * B_DK)
        q = q_ref[0, :, sl]
        k = k_ref[0, :, sl]
        v = v_ref[0, :, sl]
        gcc = gc_col[:, hh:hh + 1]
        gcr = gc_row[hh:hh + 1, :]
        tot = g_tot[:, hh:hh + 1]
        beta = gcol[:, B_HEADS + hh:B_HEADS + hh + 1]
        decay = jnp.where(incl, jnp.exp(jnp.where(incl, gcc - gcr, 0.0)), 0.0)
        kbeta = k * beta
        prod = _dot_nt(jnp.concatenate([kbeta, q], axis=0).astype(BF16), k.astype(BF16))
        a = strict * prod[:TB] * decay
        attn = prod[TB:] * decay
        e = _unit_tri_inverse_minus_identity(a, blocks_ref)
        eg = jnp.exp(gcc)
        rhs = jnp.concatenate([v * beta, kbeta * eg], axis=1)
        sol = rhs + _dot(e.astype(BF16), rhs.astype(BF16))
        u = sol[:, :B_DV]
        w = sol[:, B_DV:]
        st = st_ref[hh]
        ws_qs = _dot_nt(jnp.concatenate([w, q * eg], axis=0).astype(BF16), st.astype(BF16))
        v_new = u - ws_qs[:TB]
        o_ref[0, 0, :, sl] = ws_qs[TB:] + _dot(attn.astype(BF16), v_new.astype(BF16))
        kd = k * jnp.exp(tot - gcc)
        st_ref[hh] = st * jnp.exp(tot) + _dot_tn(v_new.astype(BF16), kd.astype(BF16))

    @pl.when(i == nt - 1)
    def _():
        sfin_ref[0, 0] = st_ref[...]


def _gdn_scan(q, k, v, gbc, gbr, s0):
    bsz, t_len, _ = q.shape
    nt = t_len // TB
    linc, tri, blocks = _gdn_consts()
    tok = pl.BlockSpec((1, TB, HD), lambda b, d, i: (b, _time_block(d, i, nt), 0))
    tok2 = pl.BlockSpec((1, 1, TB, HD), lambda b, d, i: (d, b, _time_block(d, i, nt), 0))
    st_spec = pl.BlockSpec((1, 1, B_HEADS, B_DV, B_DK), lambda b, d, i: (b, d, 0, 0, 0))
    return pl.pallas_call(
        _gdn_scan_kernel,
        grid=(bsz, 2, nt),
        in_specs=[tok, tok, tok,
                  pl.BlockSpec((1, 1, TB, LANE), lambda b, d, i: (b, d, _time_block(d, i, nt), 0)),
                  pl.BlockSpec((1, 1, SUBLANE, TB), lambda b, d, i: (b, d, 0, _time_block(d, i, nt))),
                  pl.BlockSpec((1, TB, TB), lambda b, d, i: (d, 0, 0)),
                  pl.BlockSpec((1, 2, TB, TB), lambda b, d, i: (d, 0, 0, 0)),
                  _const_spec(blocks.shape),
                  st_spec],
        out_specs=(tok2, st_spec),
        out_shape=(jax.ShapeDtypeStruct((2, bsz, t_len, HD), F32),
                   jax.ShapeDtypeStruct((bsz, 2, B_HEADS, B_DV, B_DK), F32)),
        scratch_shapes=[pltpu.VMEM((B_HEADS, B_DV, B_DK), F32)],
        compiler_params=_cp(("arbitrary", "arbitrary", "arbitrary")),
    )(q, k, v, gbc, gbr, linc, tri, blocks, s0)


def _head_norm_gate(o, g, gate, heads, width):
    parts = []
    for hh in range(heads):
        t = o[:, hh * width:(hh + 1) * width]
        ms = jnp.mean(t * t, axis=-1, keepdims=True)
        parts.append(t * lax.rsqrt(ms + EPS) * g)
    return jnp.concatenate(parts, axis=1) * gate


def _ab_out_kernel(x_ref, oaf_ref, oab_ref, obf_ref, obb_ref, ga_ref, zb_ref, mod_ref, ag_ref, bg_ref, w_ref,
                   o_ref):
    ya = _head_norm_gate(oaf_ref[0, 0] + oab_ref[0, 0], ag_ref[...], ga_ref[0], A_HEADS, A_DV)
    yb = _head_norm_gate(obf_ref[0, 0] + obb_ref[0, 0], bg_ref[...], zb_ref[0], B_HEADS, B_DV)
    y = _dot(jnp.concatenate([ya, yb], axis=1).astype(BF16), w_ref[...])
    o_ref[0] = x_ref[0] + mod_ref[0, 2:3, :] * y


def _ab_out(x, oa, ob, ga, zb, mod, per_batch, hgrn_g, gdn_g, w_out):
    bsz, t_len, _ = x.shape
    tok = lambda width: pl.BlockSpec((1, TB, width), lambda b, i: (b, i, 0))
    dir_spec = lambda d: pl.BlockSpec((1, 1, TB, HD), lambda b, i: (d, b, i, 0))
    return pl.pallas_call(
        _ab_out_kernel,
        grid=(bsz, t_len // TB),
        in_specs=[tok(D_MODEL), dir_spec(0), dir_spec(1), dir_spec(0), dir_spec(1), tok(HD), tok(HD),
                  _mod_spec(per_batch), _const_spec((1, A_DV)), _const_spec((1, B_DV)),
                  _const_spec(w_out.shape)],
        out_specs=tok(D_MODEL),
        out_shape=jax.ShapeDtypeStruct(x.shape, F32),
        compiler_params=_cp(("arbitrary", "arbitrary")),
    )(x, oa, oa, ob, ob, ga, zb, mod, hgrn_g.reshape(1, A_DV), gdn_g.reshape(1, B_DV), w_out)


def _ffn_kernel(tm, halo, row_w, final, x_ref, xp_ref, xn_ref, mod_ref, ng_ref, wa_ref, wv_ref, cw_ref, cb_ref,
                wd_ref, fg_ref, o_ref, h_ref, a_ref, acc_ref):
    i = pl.program_id(1)
    nt = pl.num_programs(1)
    sh = mod_ref[0, 3:4, :]
    sc = mod_ref[0, 4:5, :]
    g = ng_ref[...]
    x = x_ref[0]
    h_ref[halo:halo + tm, :] = _norm_mod(x, g, sc, sh).astype(BF16)
    if halo:
        h_ref[0:halo, :] = _norm_mod(xp_ref[0], g, sc, sh).astype(BF16)
        h_ref[halo + tm:, :] = _norm_mod(xn_ref[0], g, sc, sh).astype(BF16)
        keep_p = (i > 0).astype(F32)
        keep_n = (i < nt - 1).astype(F32)
    pad = SUBLANE
    zeros = jnp.zeros((pad, FT), F32)
    a_ref[0:pad, :] = zeros
    a_ref[pad + tm + 2 * halo:, :] = zeros
    col = lax.broadcasted_iota(jnp.int32, (tm, FT), 0) % row_w
    has_left = col >= 1
    has_right = col <= row_w - 2
    dys = (-1, 0, 1) if halo else (0,)
    acc_ref[...] = jnp.zeros_like(acc_ref)

    def tile(f, carry):
        a = _dot(h_ref[...], wa_ref[f])
        if halo:
            a_ref[pad:pad + halo, :] = a[0:halo] * keep_p
            a_ref[pad + halo:pad + halo + tm, :] = a[halo:halo + tm]
            a_ref[pad + halo + tm:pad + tm + 2 * halo, :] = a[halo + tm:] * keep_n
        else:
            a_ref[pad:pad + tm, :] = a
        cw = cw_ref[f]
        cols = []
        for dx in (-1, 0, 1):
            s = None
            for dy in dys:
                tap = cw[(dy + 1) * FFN_CONV + dx + 1:(dy + 1) * FFN_CONV + dx + 2, :]
                term = tap * a_ref[pl.ds(pad + halo + dy * row_w + dx, tm), :]
                s = term if s is None else s + term
            cols.append(s)
        conv = cols[1] + jnp.where(has_left, cols[0], 0.0) + jnp.where(has_right, cols[2], 0.0)
        val = _dot(h_ref[halo:halo + tm, :], wv_ref[f])
        gated = _silu(conv + cb_ref[f]) * val
        acc_ref[...] += _dot(gated.astype(BF16), wd_ref[f])
        return carry

    lax.fori_loop(0, FFN_DIM // FT, tile, 0)
    y = x + mod_ref[0, 5:6, :] * acc_ref[...]
    if final:
        ms = jnp.mean(y * y, axis=-1, keepdims=True)
        y = y * lax.rsqrt(ms + EPS) * fg_ref[...]
    o_ref[0] = y


def _ffn(x, mod, per_batch, ng, w, latent, final_g=None):
    bsz, t_len, _ = x.shape
    tm = FFN_TM if latent else t_len
    halo = GRID_W if latent else 0
    row_w = GRID_W if latent else t_len
    hb = max(halo, SUBLANE)
    per = tm // hb
    last = t_len // hb - 1
    main = pl.BlockSpec((1, tm, D_MODEL), lambda b, i: (b, i, 0))
    prev = pl.BlockSpec((1, hb, D_MODEL), lambda b, i: (b, jnp.maximum(i * per - 1, 0), 0))
    nxt = pl.BlockSpec((1, hb, D_MODEL), lambda b, i: (b, jnp.minimum((i + 1) * per, last), 0))
    final = final_g is not None
    fg = (final_g if final else jnp.ones((D_MODEL,), F32)).reshape(1, D_MODEL)
    kern = functools.partial(_ffn_kernel, tm, halo, row_w, final)
    return pl.pallas_call(
        kern,
        grid=(bsz, t_len // tm),
        in_specs=[main, prev, nxt, _mod_spec(per_batch), _const_spec((1, D_MODEL)),
                  _const_spec(w['a'].shape), _const_spec(w['v'].shape), _const_spec(w['cw'].shape),
                  _const_spec(w['cb'].shape), _const_spec(w['d'].shape), _const_spec((1, D_MODEL))],
        out_specs=main,
        out_shape=jax.ShapeDtypeStruct(x.shape, F32),
        scratch_shapes=[pltpu.VMEM((tm + 2 * halo, D_MODEL), BF16),
                        pltpu.VMEM((tm + 2 * halo + 2 * SUBLANE, FT), F32),
                        pltpu.VMEM((tm, D_MODEL), F32)],
        compiler_params=_cp(("arbitrary", "arbitrary")),
    )(x, x, x, mod, ng.reshape(1, D_MODEL), w['a'], w['v'], w['cw'], w['cb'], w['d'], fg)


def _ffn_weights(w_up, conv_w, conv_b, w_down):
    nf = FFN_DIM // FT
    tiles = lambda m: m.reshape(D_MODEL, nf, FT).transpose(1, 0, 2).astype(BF16)
    return {'a': tiles(w_up[:, :FFN_DIM]), 'v': tiles(w_up[:, FFN_DIM:]),
            'cw': conv_w.reshape(FFN_CONV * FFN_CONV, nf, FT).transpose(1, 0, 2),
            'cb': conv_b.reshape(nf, 1, FT),
            'd': w_down.reshape(nf, FT, D_MODEL).astype(BF16)}


def _c_in_kernel(x_ref, xp_ref, xn_ref, mod_ref, ng_ref, wg_ref, wx_ref, cw_ref, cb_ref, gact_ref, xc_ref,
                 ext_ref):
    i = pl.program_id(1)
    nt = pl.num_programs(1)
    sh = mod_ref[0, 0:1, :]
    sc = mod_ref[0, 1:2, :]
    g = ng_ref[...]
    hb = _norm_mod(x_ref[0], g, sc, sh).astype(BF16)
    hp = _norm_mod(xp_ref[0], g, sc, sh).astype(BF16)
    hn = _norm_mod(xn_ref[0], g, sc, sh).astype(BF16)
    gact_ref[0] = _gelu_tanh(_dot(hb, wg_ref[...]))
    ext_ref[0:CONV_HALO, :] = _dot(hp, wx_ref[...]) * (i > 0).astype(F32)
    ext_ref[CONV_HALO:CONV_HALO + TB, :] = _dot(hb, wx_ref[...])
    ext_ref[CONV_HALO + TB:, :] = _dot(hn, wx_ref[...]) * (i < nt - 1).astype(F32)
    xc_ref[0] = _conv_tokens(ext_ref, cw_ref, C_CONV) + cb_ref[...]


def _c_in(x, mod, per_batch, ng, wg, wx, conv_w, conv_b):
    bsz, t_len, _ = x.shape
    main, prev, nxt = _halo_specs(t_len, D_MODEL)
    tok = pl.BlockSpec((1, TB, C_WIDTH), lambda b, i: (b, i, 0))
    return pl.pallas_call(
        _c_in_kernel,
        grid=(bsz, t_len // TB),
        in_specs=[main, prev, nxt, _mod_spec(per_batch), _const_spec((1, D_MODEL)), _const_spec(wg.shape),
                  _const_spec(wx.shape), _const_spec(conv_w.shape), _const_spec((1, C_WIDTH))],
        out_specs=(tok, tok),
        out_shape=(jax.ShapeDtypeStruct((bsz, t_len, C_WIDTH), F32),) * 2,
        scratch_shapes=[pltpu.VMEM((TB + 2 * CONV_HALO, C_WIDTH), F32)],
        compiler_params=_cp(("arbitrary", "arbitrary")),
    )(x, x, x, mod, ng.reshape(1, D_MODEL), wg, wx, conv_w, conv_b.reshape(1, C_WIDTH))


def _rglru_scan_kernel(xc_ref, gw_ref, gb_ref, lam_ref, h0_ref, o_ref, hfin_ref, a_ref, u_ref, carry_ref):
    d = pl.program_id(1)
    i = pl.program_id(2)
    nt = pl.num_programs(2)

    @pl.when(i == 0)
    def _():
        carry_ref[...] = h0_ref[0, 0]

    for hh in range(C_HEADS):
        sl = slice(hh * C_BLOCK, (hh + 1) * C_BLOCK)
        xh = xc_ref[0, :, sl]
        gates = _sigmoid(_dot(xh.astype(BF16), gw_ref[0, hh]) + gb_ref[0, hh])
        r = gates[:, :C_BLOCK]
        gi = gates[:, C_BLOCK:]
        log_a = (-C_GATE_C) * r * _softplus(-lam_ref[0, :, sl])
        a_ref[:, sl] = jnp.exp(log_a)
        u_ref[:, sl] = jnp.sqrt(1.0 - jnp.exp(2.0 * log_a)) * (gi * xh)

    n_grp = TB // SUBLANE
    row = lax.broadcasted_iota(jnp.int32, (SUBLANE, C_WIDTH), 0)

    def scan_block(fwd):
        def group(n, carry):
            grp = n if fwd else n_grp - 1 - n
            start = pl.multiple_of(grp * SUBLANE, SUBLANE)
            a = a_ref[pl.ds(start, SUBLANE), :]
            u = u_ref[pl.ds(start, SUBLANE), :]
            s = 1
            while s < SUBLANE:
                has_prev = (row >= s) if fwd else (row < SUBLANE - s)
                shift = s if fwd else SUBLANE - s
                a_s = jnp.where(has_prev, pltpu.roll(a, shift, axis=0), 1.0)
                u_s = jnp.where(has_prev, pltpu.roll(u, shift, axis=0), 0.0)
                a, u = a * a_s, a * u_s + u
                s *= 2
            h = a * carry + u
            o_ref[0, 0, pl.ds(start, SUBLANE), :] = h
            return h[SUBLANE - 1:SUBLANE, :] if fwd else h[0:1, :]

        carry_ref[...] = lax.fori_loop(0, n_grp, group, carry_ref[...])

    @pl.when(d == 0)
    def _():
        scan_block(True)

    @pl.when(d == 1)
    def _():
        scan_block(False)

    @pl.when(i == nt - 1)
    def _():
        hfin_ref[0, 0] = carry_ref[...]


def _rglru_scan(xc, gate_w, gate_b, lam, h0):
    bsz, t_len, _ = xc.shape
    nt = t_len // TB
    st_spec = pl.BlockSpec((1, 1, 1, C_WIDTH), lambda b, d, i: (b, d, 0, 0))
    return pl.pallas_call(
        _rglru_scan_kernel,
        grid=(bsz, 2, nt),
        in_specs=[pl.BlockSpec((1, TB, C_WIDTH), lambda b, d, i: (b, _time_block(d, i, nt), 0)),
                  pl.BlockSpec((1,) + gate_w.shape[1:], lambda b, d, i: (d, 0, 0, 0)),
                  pl.BlockSpec((1,) + gate_b.shape[1:], lambda b, d, i: (d, 0, 0, 0)),
                  pl.BlockSpec((1, 1, C_WIDTH), lambda b, d, i: (d, 0, 0)),
                  st_spec],
        out_specs=(pl.BlockSpec((1, 1, TB, C_WIDTH), lambda b, d, i: (d, b, _time_block(d, i, nt), 0)), st_spec),
        out_shape=(jax.ShapeDtypeStruct((2, bsz, t_len, C_WIDTH), F32),
                   jax.ShapeDtypeStruct((bsz, 2, 1, C_WIDTH), F32)),
        scratch_shapes=[pltpu.VMEM((TB, C_WIDTH), F32), pltpu.VMEM((TB, C_WIDTH), F32),
                        pltpu.VMEM((1, C_WIDTH), F32)],
        compiler_params=_cp(("arbitrary", "arbitrary", "arbitrary")),
    )(xc, gate_w, gate_b, lam, h0)


def _c_out_kernel(x_ref, hf_ref, hb_ref, gact_ref, mod_ref, w_ref, o_ref):
    y = _dot((gact_ref[0] * (hf_ref[0, 0] + hb_ref[0, 0])).astype(BF16), w_ref[...])
    o_ref[0] = x_ref[0] + mod_ref[0, 2:3, :] * y


def _c_out(x, hdir, gact, mod, per_batch, w_out):
    bsz, t_len, _ = x.shape
    tok = pl.BlockSpec((1, TB, D_MODEL), lambda b, i: (b, i, 0))
    dir_spec = lambda d: pl.BlockSpec((1, 1, TB, C_WIDTH), lambda b, i: (d, b, i, 0))
    return pl.pallas_call(
        _c_out_kernel,
        grid=(bsz, t_len // TB),
        in_specs=[tok, dir_spec(0), dir_spec(1), tok, _mod_spec(per_batch), _const_spec(w_out.shape)],
        out_specs=tok,
        out_shape=jax.ShapeDtypeStruct(x.shape, F32),
        compiler_params=_cp(("arbitrary", "arbitrary")),
    )(x, hdir, hdir, gact, mod, w_out)


def _mod_rows(mods_l, bsz):
    m = mods_l.reshape(mods_l.shape[0], 6, D_MODEL)
    m = jnp.pad(m, ((0, 0), (0, SUBLANE - 6), (0, 0)))
    return m[:bsz], m[bsz:bsz + 1]


def kernel(x, c, ctx, c_ctx, mod_w, mod_b, norm_mix_g, norm_ffn_g, final_norm_g, ab_w_in, hgrn_lb, hgrn_norm_g, gdn_conv_w, gdn_a_log, gdn_dt_bias, gdn_norm_g, ab_w_out, c_w_in, c_conv_w, c_conv_b, c_gate_w, c_gate_b, c_lambda, c_w_out, ffn_w_up, ffn_conv_w, ffn_conv_b, ffn_w_down):
    bsz = x.shape[0]
    rows = 2 * SUBLANE
    cc = jnp.concatenate([c, c_ctx[None, :], jnp.zeros((rows - bsz - 1, D_MODEL), F32)], axis=0)
    mods = _modulation(cc, mod_w, mod_b)
    xc = ctx
    for l in range(DEPTH):
        last = l == DEPTH - 1
        mod_x, mod_c = _mod_rows(mods[l], bsz)
        ffn_w = _ffn_weights(ffn_w_up[l], ffn_conv_w[l], ffn_conv_b[l], ffn_w_down[l])
        if l % 2 == 0:
            e = l // 2
            w_in = ab_w_in[e]
            o = 5 * HD
            w = {'a': w_in[:, :o].astype(BF16), 'qkv': w_in[:, o:o + QKV].astype(BF16),
                 'z': w_in[:, o + QKV:o + QKV + HD].astype(BF16),
                 'alpha': w_in[:, o + QKV + HD:o + QKV + HD + 2 * B_HEADS],
                 'beta': w_in[:, o + QKV + HD + 2 * B_HEADS:]}
            w_out = ab_w_out[e].astype(BF16)
            feats = []
            for seq, mod, per_batch in ((xc, mod_c, False), (x, mod_x, True)):
                feats.append(_ab_in(l, seq, mod, per_batch, norm_mix_g[l], w, hgrn_lb, gdn_conv_w[e],
                                    gdn_a_log[e], gdn_dt_bias[e]))
            outs = []
            sa = jnp.zeros((bsz, 2, A_HEADS, A_DV, A_DK), F32)
            sb = jnp.zeros((bsz, 2, B_HEADS, B_DV, B_DK), F32)
            for qa, ka, lfa, va, ga, qb, kb, vb, zb, gbc, gbr in feats:
                oa, sa = _hgrn_scan(qa, ka, lfa, va, sa)
                ob, sb = _gdn_scan(qb, kb, vb, gbc, gbr, sb)
                outs.append((oa, ob, ga, zb))
            oa, ob, ga, zb = outs[1]
            x = _ab_out(x, oa, ob, ga, zb, mod_x, True, hgrn_norm_g[e], gdn_norm_g[e], w_out)
            if not last:
                oa, ob, ga, zb = outs[0]
                xc = _ab_out(xc, oa, ob, ga, zb, mod_c, False, hgrn_norm_g[e], gdn_norm_g[e], w_out)
        else:
            o = l // 2
            wg = c_w_in[o][:, :C_WIDTH].astype(BF16)
            wx = c_w_in[o][:, C_WIDTH:].astype(BF16)
            gate_w = c_gate_w[o].astype(BF16)
            gate_b = c_gate_b[o].reshape(2, C_HEADS, 1, 2 * C_BLOCK)
            lam = c_lambda[o].reshape(2, 1, C_WIDTH)
            h_state = jnp.zeros((bsz, 2, 1, C_WIDTH), F32)
            res = []
            for seq, mod, per_batch in ((xc, mod_c, False), (x, mod_x, True)):
                gact, xconv = _c_in(seq, mod, per_batch, norm_mix_g[l], wg, wx, c_conv_w[o], c_conv_b[o])
                hdir, h_state = _rglru_scan(xconv, gate_w, gate_b, lam, h_state)
                res.append((hdir, gact))
            x = _c_out(x, res[1][0], res[1][1], mod_x, True, c_w_out[o].astype(BF16))
            if not last:
                xc = _c_out(xc, res[0][0], res[0][1], mod_c, False, c_w_out[o].astype(BF16))
        if not last:
            xc = _ffn(xc, mod_c, False, norm_ffn_g[l], ffn_w, latent=False)
        x = _ffn(x, mod_x, True, norm_ffn_g[l], ffn_w, latent=True, final_g=final_norm_g if last else None)
    return x
```

```python
import functools
import math

import numpy as np
import jax
import jax.numpy as jnp
from jax import lax
from jax.experimental import pallas as pl
from jax.experimental.pallas import tpu as pltpu

F32 = jnp.float32
BF16 = jnp.bfloat16

D_MODEL = 1024
DEPTH = 2
GRID_W = 64
EPS = 1e-6
A_HEADS = 4
A_DK = 128
A_DV = 128
B_HEADS = 4
B_DK = 128
B_DV = 128
B_CONV = 4
C_WIDTH = D_MODEL
C_HEADS = 4
C_BLOCK = C_WIDTH // C_HEADS
C_CONV = 4
C_GATE_C = 8.0
FFN_DIM = 2816
FFN_CONV = 3

HD = A_HEADS * A_DK
QKV = 2 * B_HEADS * B_DK + B_HEADS * B_DV

LANE = 128
SUBLANE = 8
VMEM_LIMIT = 56 * 1024 * 1024

TB = 256
SUB = 16
LEVELS = (16, 32, 64, 128)
FT = 256
FFN_TM = 1024
CONV_HALO = 8


def _cp(sem):
    return pltpu.CompilerParams(dimension_semantics=sem, vmem_limit_bytes=VMEM_LIMIT)


def _dot(a, b):
    return jnp.dot(a, b, preferred_element_type=F32)


def _dot_nt(a, b):
    return lax.dot_general(a, b, (((1,), (1,)), ((), ())), preferred_element_type=F32)


def _dot_tn(a, b):
    return lax.dot_general(a, b, (((0,), (0,)), ((), ())), preferred_element_type=F32)


def _sigmoid(x):
    return 1.0 / (1.0 + jnp.exp(-x))


def _silu(x):
    return x * _sigmoid(x)


def _softplus(x):
    return jnp.maximum(x, 0.0) + jnp.log1p(jnp.exp(-jnp.abs(x)))


def _gelu_tanh(x):
    c = math.sqrt(2.0 / math.pi)
    return 0.5 * x * (1.0 + jnp.tanh(c * (x + 0.044715 * (x * x * x))))


def _hi_lo(x):
    hi = x.astype(BF16)
    lo = (x - hi.astype(F32)).astype(BF16)
    return hi, lo


def _norm_mod(x, g, sc, sh):
    ms = jnp.mean(x * x, axis=-1, keepdims=True)
    return (x * lax.rsqrt(ms + EPS)) * g * (1.0 + sc) + sh


def _const_spec(shape):
    nd = len(shape)
    return pl.BlockSpec(shape, lambda *_: (0,) * nd, pipeline_mode=pl.Buffered(1))


MOD_TN = 1024


def _mod_kernel(c_ref, w_ref, b_ref, o_ref):
    s = _silu(c_ref[...])
    w = w_ref[0]
    s_hi, s_lo = _hi_lo(s)
    w_hi, w_lo = _hi_lo(w)
    acc = _dot(s_hi, w_hi) + _dot(s_hi, w_lo) + _dot(s_lo, w_hi)
    o_ref[0] = acc + b_ref[0]


def _modulation(cc, mod_w, mod_b):
    rows = cc.shape[0]
    n = mod_w.shape[-1]
    return pl.pallas_call(
        _mod_kernel,
        grid=(DEPTH, n // MOD_TN),
        in_specs=[pl.BlockSpec((rows, D_MODEL), lambda l, j: (0, 0)),
                  pl.BlockSpec((1, D_MODEL, MOD_TN), lambda l, j: (l, 0, j)),
                  pl.BlockSpec((1, 1, MOD_TN), lambda l, j: (l, 0, j))],
        out_specs=pl.BlockSpec((1, rows, MOD_TN), lambda l, j: (l, 0, j)),
        out_shape=jax.ShapeDtypeStruct((DEPTH, rows, n), F32),
        compiler_params=_cp(("arbitrary", "arbitrary")),
    )(cc, mod_w, mod_b.reshape(DEPTH, 1, n))


def _halo_specs(t_len, width):
    per = TB // CONV_HALO
    last = t_len // CONV_HALO - 1
    main = pl.BlockSpec((1, TB, width), lambda b, i: (b, i, 0))
    prev = pl.BlockSpec((1, CONV_HALO, width), lambda b, i: (b, jnp.maximum(i * per - 1, 0), 0))
    nxt = pl.BlockSpec((1, CONV_HALO, width), lambda b, i: (b, jnp.minimum((i + 1) * per, last), 0))
    return main, prev, nxt


def _mod_spec(per_batch):
    if per_batch:
        return pl.BlockSpec((1, SUBLANE, D_MODEL), lambda b, *_: (b, 0, 0))
    return pl.BlockSpec((1, SUBLANE, D_MODEL), lambda b, *_: (0, 0, 0))


def _conv_tokens(ext_ref, w_ref, width):
    left = width // 2
    acc = None
    for j in range(width):
        term = w_ref[j:j + 1, :] * ext_ref[pl.ds(CONV_HALO + j - left, TB), :]
        acc = term if acc is None else acc + term
    return acc


def _ab_in_kernel(layer, x_ref, xp_ref, xn_ref, mod_ref, ng_ref, wa_ref, wqkv_ref, wz_ref, wab_ref, wabt_ref,
                  lbp_ref, convw_ref, coef_ref, coeft_ref,
                  qa_ref, ka_ref, lfa_ref, va_ref, ga_ref, qb_ref, kb_ref, vb_ref, zb_ref, gbc_ref, gbr_ref,
                  ext_ref):
    i = pl.program_id(1)
    nt = pl.num_programs(1)
    sh = mod_ref[0, 0:1, :]
    sc = mod_ref[0, 1:2, :]
    g = ng_ref[...]
    hb = _norm_mod(x_ref[0], g, sc, sh).astype(BF16)
    hp = _norm_mod(xp_ref[0], g, sc, sh).astype(BF16)
    hn = _norm_mod(xn_ref[0], g, sc, sh).astype(BF16)
    keep_p = (i > 0).astype(F32)
    keep_n = (i < nt - 1).astype(F32)

    def lower_bound(d):
        rws = [lbp_ref[d * (DEPTH + 1) + r:d * (DEPTH + 1) + r + 1, :] for r in range(DEPTH + 1)]
        mx = functools.reduce(jnp.maximum, rws)
        es = [jnp.exp(r - mx) for r in rws]
        return functools.reduce(jnp.add, es[:layer + 1]) / functools.reduce(jnp.add, es)

    def a_proj(group):
        return lambda: _dot(hb, wa_ref[:, group * HD:(group + 1) * HD])

    def put_q(p):
        qa_ref[0] = _silu(p) * (A_DK ** -0.5)

    def put_forget(d):
        def put(p):
            lb_d = lower_bound(d)
            ka_ref[d, 0] = (1.0 - lb_d) * _sigmoid(-p)
            lfa_ref[d, 0] = jnp.log(lb_d + (1.0 - lb_d) * _sigmoid(p))
        return put

    def put_v(p):
        va_ref[0] = p

    def put_gate(p):
        ga_ref[0] = _silu(p)

    def conv_proj(part):
        def run():
            w = wqkv_ref[:, part * HD:(part + 1) * HD]
            slot = part % 2
            ext_ref[slot, 0:CONV_HALO, :] = _dot(hp, w) * keep_p
            ext_ref[slot, CONV_HALO:CONV_HALO + TB, :] = _dot(hb, w)
            ext_ref[slot, CONV_HALO + TB:, :] = _dot(hn, w) * keep_n
        return run

    def put_conv(part):
        def put(_):
            lo = part * HD
            t_all = _silu(_conv_tokens(ext_ref.at[part % 2], convw_ref.at[:, lo:lo + HD], B_CONV))
            if part == 2:
                vb_ref[0] = t_all
                return
            ref, scale = ((qb_ref, B_DK ** -0.5), (kb_ref, 1.0))[part]
            for hh in range(B_HEADS):
                t = t_all[:, hh * B_DK:(hh + 1) * B_DK]
                ss = jnp.sum(t * t, axis=-1, keepdims=True)
                ref[0, :, hh * B_DK:(hh + 1) * B_DK] = t * (lax.rsqrt(ss + EPS) * scale)
        return put

    def put_z(p):
        zb_ref[0] = _silu(p)

    def gate_proj():
        return _dot(hb, wab_ref[...]), _dot_nt(wabt_ref[...], hb)

    def put_gates(p):
        ab, abt = p
        lane = lax.broadcasted_iota(jnp.int32, ab.shape, 1)
        is_g = (lane % SUBLANE) < B_HEADS
        act = jnp.where(is_g, -jnp.exp(coef_ref[0:1, :]) * _softplus(ab + coef_ref[1:2, :]), _sigmoid(ab))
        row = lax.broadcasted_iota(jnp.int32, abt.shape, 0)
        is_g_t = (row % SUBLANE) < B_HEADS
        act_t = jnp.where(is_g_t, -jnp.exp(coeft_ref[:, 0:1]) * _softplus(abt + coeft_ref[:, 1:2]), _sigmoid(abt))
        col = lax.broadcasted_iota(jnp.int32, (TB, LANE), 1)
        for d in range(2):
            shifted = act if d == 0 else pltpu.roll(act, LANE - SUBLANE, axis=1)
            gbc_ref[0, d] = jnp.where(col < SUBLANE, shifted, 0.0)
            gbr_ref[0, d] = act_t[d * SUBLANE:(d + 1) * SUBLANE, :]

    stages = [(a_proj(0), put_q), (a_proj(1), put_forget(0)), (a_proj(2), put_forget(1)), (a_proj(3), put_v),
              (a_proj(4), put_gate), (conv_proj(0), put_conv(0)), (conv_proj(1), put_conv(1)),
              (conv_proj(2), put_conv(2)), (lambda: _dot(hb, wz_ref[...]), put_z), (gate_proj, put_gates)]
    pending = stages[0][0]()
    for s, (_, consume) in enumerate(stages):
        nxt = stages[s + 1][0]() if s + 1 < len(stages) else None
        consume(pending)
        pending = nxt


def _ab_in(layer, x, mod, per_batch, ng, w, hgrn_lb, conv_w, a_log, dt_bias):
    bsz, t_len, _ = x.shape
    nt = t_len // TB
    main, prev, nxt = _halo_specs(t_len, D_MODEL)
    tok = lambda width: pl.BlockSpec((1, TB, width), lambda b, i: (b, i, 0))
    tok2 = lambda width: pl.BlockSpec((2, 1, TB, width), lambda b, i: (0, b, i, 0))
    sds = lambda *s: jax.ShapeDtypeStruct(s, F32)
    out_shape = (sds(bsz, t_len, HD), sds(2, bsz, t_len, HD), sds(2, bsz, t_len, HD), sds(bsz, t_len, HD),
                 sds(bsz, t_len, HD), sds(bsz, t_len, HD), sds(bsz, t_len, HD), sds(bsz, t_len, HD),
                 sds(bsz, t_len, HD), sds(bsz, 2, t_len, LANE), sds(bsz, 2, SUBLANE, t_len))
    out_specs = (tok(HD), tok2(HD), tok2(HD), tok(HD), tok(HD), tok(HD), tok(HD), tok(HD), tok(HD),
                 pl.BlockSpec((1, 2, TB, LANE), lambda b, i: (b, 0, i, 0)),
                 pl.BlockSpec((1, 2, SUBLANE, TB), lambda b, i: (b, 0, 0, i)))
    w_al, w_be = w['alpha'], w['beta']
    cols = [w_al[:, 0:4], w_be[:, 0:4], w_al[:, 4:8], w_be[:, 4:8]]
    wab = jnp.concatenate(cols, axis=1)
    wab_p = jnp.pad(wab, ((0, 0), (0, LANE - 2 * SUBLANE))).astype(BF16)
    wabt = wab.T.astype(BF16)
    zero4 = jnp.zeros((B_HEADS,), F32)
    a_flat = jnp.concatenate([a_log[0], zero4, a_log[1], zero4])
    d_flat = jnp.concatenate([dt_bias[0], zero4, dt_bias[1], zero4])
    coef = jnp.pad(jnp.stack([a_flat, d_flat]), ((0, 0), (0, LANE - 2 * SUBLANE)))
    coeft = jnp.stack([a_flat, d_flat], axis=1)
    hgrn_lb = hgrn_lb.reshape(2 * (DEPTH + 1), HD)
    args = (x, x, x, mod, ng.reshape(1, D_MODEL), w['a'], w['qkv'], w['z'], wab_p, wabt, hgrn_lb, conv_w, coef,
            coeft)
    in_specs = [main, prev, nxt, _mod_spec(per_batch), _const_spec((1, D_MODEL)), _const_spec(w['a'].shape),
                _const_spec(w['qkv'].shape), _const_spec(w['z'].shape), _const_spec(wab_p.shape),
                _const_spec(wabt.shape), _const_spec(hgrn_lb.shape), _const_spec(conv_w.shape),
                _const_spec(coef.shape), _const_spec(coeft.shape)]
    return pl.pallas_call(
        functools.partial(_ab_in_kernel, layer),
        grid=(bsz, nt),
        in_specs=in_specs,
        out_specs=out_specs,
        out_shape=out_shape,
        scratch_shapes=[pltpu.VMEM((2, TB + 2 * CONV_HALO, HD), F32)],
        compiler_params=_cp(("arbitrary", "arbitrary")),
    )(*args)


def _both_dirs(stack):
    return np.stack([stack, stack[:, ::-1, ::-1]])


@functools.lru_cache(maxsize=None)
def _hgrn_consts():
    t = np.arange(TB)
    i, j = t[:, None], t[None, :]
    sums = [(i // SUB == j // SUB) & (j <= i)]
    masks = [(i // SUB == j // SUB) & (j <= i)]
    for m in LEVELS:
        seg = 2 * m
        same = (i // seg) == (j // seg)
        mid = (i // seg) * seg + m - 1
        second = (i % seg) >= m
        sums.append(same & np.where(second, (j > mid) & (j <= i), (j > i) & (j <= mid)))
        masks.append(same & second & ((j % seg) < m))
    sums.append(j <= i)
    sums.append(j > i)
    sums = _both_dirs(np.stack(sums).astype(np.float32))
    masks = _both_dirs(np.stack(masks).astype(np.float32))
    n = sums.shape[1]
    return jnp.asarray(sums.reshape(2, n * TB, TB), BF16), jnp.asarray(masks, F32)


@functools.lru_cache(maxsize=None)
def _gdn_consts():
    t = np.arange(TB)
    i, j = t[:, None], t[None, :]
    tri = _both_dirs(np.stack([j <= i, j < i]).astype(np.float32))
    blocks = [(i // SUB) == (j // SUB)]
    for m in LEVELS:
        seg = 2 * m
        blocks.append(((i // seg) == (j // seg)) & ((i // m) != (j // m)))
    blocks = np.stack(blocks).astype(np.float32)
    return jnp.asarray(tri[:, 0], BF16), jnp.asarray(tri, F32), jnp.asarray(blocks, F32)


def _time_block(d, i, nt):
    return i + d * (nt - 1 - 2 * i)


def _hgrn_scan_kernel(q_ref, k_ref, lf_ref, v_ref, sums_ref, masks_ref, s0_ref, o_ref, sfin_ref, st_ref):
    i = pl.program_id(2)
    nt = pl.num_programs(2)

    @pl.when(i == 0)
    def _():
        st_ref[...] = s0_ref[0, 0]

    n_lv = len(LEVELS)
    heads = range(A_HEADS)
    sls = [slice(hh * A_DK, (hh + 1) * A_DK) for hh in heads]
    xxs = []
    for hh in heads:
        lf_hi, lf_lo = _hi_lo(lf_ref[0, 0, :, sls[hh]])
        xx = _dot(sums_ref[0], jnp.concatenate([lf_hi, lf_lo], axis=1))
        xxs.append(xx[:, :A_DK] + xx[:, A_DK:])
    attns = []
    for hh in heads:
        x0 = xxs[hh][0:TB]
        raw = _dot_nt((q_ref[0, :, sls[hh]] * jnp.exp(x0)).astype(BF16),
                      (k_ref[0, 0, :, sls[hh]] * jnp.exp(-x0)).astype(BF16))
        attns.append(jnp.where(masks_ref[0, 0] > 0.0, raw, 0.0))
    for lv in range(1, n_lv + 1):
        for hh in heads:
            e = jnp.exp(xxs[hh][lv * TB:(lv + 1) * TB])
            attns[hh] = attns[hh] + masks_ref[0, lv] * _dot_nt((q_ref[0, :, sls[hh]] * e).astype(BF16),
                                                               (k_ref[0, 0, :, sls[hh]] * e).astype(BF16))
    sts = [st_ref[hh] for hh in heads]
    for hh in heads:
        e_in = jnp.exp(xxs[hh][(n_lv + 1) * TB:(n_lv + 2) * TB])
        o_ref[0, 0, :, sls[hh]] = (_dot(attns[hh].astype(BF16), v_ref[0, :, sls[hh]].astype(BF16))
                                   + _dot_nt((q_ref[0, :, sls[hh]] * e_in).astype(BF16), sts[hh].astype(BF16)))
    for hh in heads:
        e_out = jnp.exp(xxs[hh][(n_lv + 2) * TB:(n_lv + 3) * TB])
        gl = jnp.exp(jnp.sum(lf_ref[0, 0, :, sls[hh]], axis=0, keepdims=True))
        st_ref[hh] = sts[hh] * gl + _dot_tn(v_ref[0, :, sls[hh]].astype(BF16),
                                            (k_ref[0, 0, :, sls[hh]] * e_out).astype(BF16))

    @pl.when(i == nt - 1)
    def _():
        sfin_ref[0, 0] = st_ref[...]


def _hgrn_scan(q, k2, lf2, v, s0):
    bsz, t_len, _ = q.shape
    nt = t_len // TB
    sums, masks = _hgrn_consts()
    tok = pl.BlockSpec((1, TB, HD), lambda b, d, i: (b, _time_block(d, i, nt), 0))
    tok2 = pl.BlockSpec((1, 1, TB, HD), lambda b, d, i: (d, b, _time_block(d, i, nt), 0))
    st_spec = pl.BlockSpec((1, 1, A_HEADS, A_DV, A_DK), lambda b, d, i: (b, d, 0, 0, 0))
    return pl.pallas_call(
        _hgrn_scan_kernel,
        grid=(bsz, 2, nt),
        in_specs=[tok, tok2, tok2, tok,
                  pl.BlockSpec((1,) + sums.shape[1:], lambda b, d, i: (d, 0, 0)),
                  pl.BlockSpec((1,) + masks.shape[1:], lambda b, d, i: (d, 0, 0, 0)),
                  st_spec],
        out_specs=(tok2, st_spec),
        out_shape=(jax.ShapeDtypeStruct((2, bsz, t_len, HD), F32),
                   jax.ShapeDtypeStruct((bsz, 2, A_HEADS, A_DV, A_DK), F32)),
        scratch_shapes=[pltpu.VMEM((A_HEADS, A_DV, A_DK), F32)],
        compiler_params=_cp(("arbitrary", "arbitrary", "arbitrary")),
    )(q, k2, lf2, v, sums, masks, s0)


def _unit_tri_inverse_minus_identity(a_list, blocks_ref):
    b16 = lambda x: x.astype(BF16)
    ps = [a * blocks_ref[0] for a in a_list]
    es = [-p for p in ps]
    width = 1
    while 2 * width < SUB:
        ps = [_dot(b16(p), b16(p)) for p in ps]
        es = [e + p + _dot(b16(e), b16(p)) for e, p in zip(es, ps)]
        width *= 2
    for lv in range(len(LEVELS)):
        bms = [a * blocks_ref[lv + 1] for a in a_list]
        ys = [bm + _dot(b16(e), b16(bm)) for e, bm in zip(es, bms)]
        es = [e - (y + _dot(b16(y), b16(e))) for e, y in zip(es, ys)]
    return es


def _gdn_scan_kernel(q_ref, k_ref, v_ref, gbc_ref, gbr_ref, linc_ref, tri_ref, blocks_ref, s0_ref,
                     o_ref, sfin_ref, st_ref):
    i = pl.program_id(2)
    nt = pl.num_programs(2)

    @pl.when(i == 0)
    def _():
        st_ref[...] = s0_ref[0, 0]

    gcol = gbc_ref[0, 0]
    grow = gbr_ref[0, 0]
    linc = linc_ref[0]
    c_hi, c_lo = _hi_lo(gcol)
    gc_col = _dot(linc, jnp.concatenate([c_hi, c_lo], axis=1))
    gc_col = gc_col[:, :LANE] + gc_col[:, LANE:]
    r_hi, r_lo = _hi_lo(grow)
    gc_row = _dot_nt(jnp.concatenate([r_hi, r_lo], axis=0), linc)
    gc_row = gc_row[:SUBLANE] + gc_row[SUBLANE:]
    g_tot = jnp.sum(gcol, axis=0, keepdims=True)
    incl = tri_ref[0, 0] > 0.0
    strict = tri_ref[0, 1]

    heads = range(B_HEADS)
    sls = [slice(hh * B_DK, (hh + 1) * B_DK) for hh in heads]
    gccs = [gc_col[:, hh:hh + 1] for hh in heads]
    tots = [g_tot[:, hh:hh + 1] for hh in heads]
    betas = [gcol[:, B_HEADS + hh:B_HEADS + hh + 1] for hh in heads]
    a_list, attns, kbetas = [], [], []
    for hh in heads:
        q = q_ref[0, :, sls[hh]]
        k = k_ref[0, :, sls[hh]]
        gcr = gc_row[hh:hh + 1, :]
        decay = jnp.where(incl, jnp.exp(jnp.where(incl, gccs[hh] - gcr, 0.0)), 0.0)
        kbeta = k * betas[hh]
        prod = _dot_nt(jnp.concatenate([kbeta, q], axis=0).astype(BF16), k.astype(BF16))
        a_list.append(strict * prod[:TB] * decay)
        attns.append(prod[TB:] * decay)
        kbetas.append(kbeta)
    es = _unit_tri_inverse_minus_identity(a_list, blocks_ref)
    egs = [jnp.exp(gccs[hh]) for hh in heads]
    sols = []
    for hh in heads:
        rhs = jnp.concatenate([v_ref[0, :, sls[hh]] * betas[hh], kbetas[hh] * egs[hh]], axis=1)
        sols.append(rhs + _dot(es[hh].astype(BF16), rhs.astype(BF16)))
    sts = [st_ref[hh] for hh in heads]
    ws_qs = [_dot_nt(jnp.concatenate([sols[hh][:, B_DV:], q_ref[0, :, sls[hh]] * egs[hh]], axis=0).astype(BF16),
                     sts[hh].astype(BF16)) for hh in heads]
    v_news = [sols[hh][:, :B_DV] - ws_qs[hh][:TB] for hh in heads]
    for hh in heads:
        o_ref[0, 0, :, sls[hh]] = ws_qs[hh][TB:] + _dot(attns[hh].astype(BF16), v_news[hh].astype(BF16))
    for hh in heads:
        kd = k_ref[0, :, sls[hh]] * jnp.exp(tots[hh] - gccs[hh])
        st_ref[hh] = sts[hh] * jnp.exp(tots[hh]) + _dot_tn(v_news[hh].astype(BF16), kd.astype(BF16))

    @pl.when(i == nt - 1)
    def _():
        sfin_ref[0, 0] = st_ref[...]


def _gdn_scan(q, k, v, gbc, gbr, s0):
    bsz, t_len, _ = q.shape
    nt = t_len // TB
    linc, tri, blocks = _gdn_consts()
    tok = pl.BlockSpec((1, TB, HD), lambda b, d, i: (b, _time_block(d, i, nt), 0))
    tok2 = pl.BlockSpec((1, 1, TB, HD), lambda b, d, i: (d, b, _time_block(d, i, nt), 0))
    st_spec = pl.BlockSpec((1, 1, B_HEADS, B_DV, B_DK), lambda b, d, i: (b, d, 0, 0, 0))
    return pl.pallas_call(
        _gdn_scan_kernel,
        grid=(bsz, 2, nt),
        in_specs=[tok, tok, tok,
                  pl.BlockSpec((1, 1, TB, LANE), lambda b, d, i: (b, d, _time_block(d, i, nt), 0)),
                  pl.BlockSpec((1, 1, SUBLANE, TB), lambda b, d, i: (b, d, 0, _time_block(d, i, nt))),
                  pl.BlockSpec((1, TB, TB), lambda b, d, i: (d, 0, 0)),
                  pl.BlockSpec((1, 2, TB, TB), lambda b, d, i: (d, 0, 0, 0)),
                  _const_spec(blocks.shape),
                  st_spec],
        out_specs=(tok2, st_spec),
        out_shape=(jax.ShapeDtypeStruct((2, bsz, t_len, HD), F32),
                   jax.ShapeDtypeStruct((bsz, 2, B_HEADS, B_DV, B_DK), F32)),
        scratch_shapes=[pltpu.VMEM((B_HEADS, B_DV, B_DK), F32)],
        compiler_params=_cp(("arbitrary", "arbitrary", "arbitrary")),
    )(q, k, v, gbc, gbr, linc, tri, blocks, s0)


def _head_norm_gate(o, g, gate, heads, width):
    parts = []
    for hh in range(heads):
        t = o[:, hh * width:(hh + 1) * width]
        ms = jnp.mean(t * t, axis=-1, keepdims=True)
        parts.append(t * lax.rsqrt(ms + EPS) * g)
    return jnp.concatenate(parts, axis=1) * gate


def _ab_out_kernel(x_ref, oaf_ref, oab_ref, obf_ref, obb_ref, ga_ref, zb_ref, mod_ref, ag_ref, bg_ref, w_ref,
                   o_ref):
    ya = _head_norm_gate(oaf_ref[0, 0] + oab_ref[0, 0], ag_ref[...], ga_ref[0], A_HEADS, A_DV)
    yb = _head_norm_gate(obf_ref[0, 0] + obb_ref[0, 0], bg_ref[...], zb_ref[0], B_HEADS, B_DV)
    y = _dot(jnp.concatenate([ya, yb], axis=1).astype(BF16), w_ref[...])
    o_ref[0] = x_ref[0] + mod_ref[0, 2:3, :] * y


def _ab_out(x, oa, ob, ga, zb, mod, per_batch, hgrn_g, gdn_g, w_out):
    bsz, t_len, _ = x.shape
    tok = lambda width: pl.BlockSpec((1, TB, width), lambda b, i: (b, i, 0))
    dir_spec = lambda d: pl.BlockSpec((1, 1, TB, HD), lambda b, i: (d, b, i, 0))
    return pl.pallas_call(
        _ab_out_kernel,
        grid=(bsz, t_len // TB),
        in_specs=[tok(D_MODEL), dir_spec(0), dir_spec(1), dir_spec(0), dir_spec(1), tok(HD), tok(HD),
                  _mod_spec(per_batch), _const_spec((1, A_DV)), _const_spec((1, B_DV)),
                  _const_spec(w_out.shape)],
        out_specs=tok(D_MODEL),
        out_shape=jax.ShapeDtypeStruct(x.shape, F32),
        compiler_params=_cp(("arbitrary", "arbitrary")),
    )(x, oa, oa, ob, ob, ga, zb, mod, hgrn_g.reshape(1, A_DV), gdn_g.reshape(1, B_DV), w_out)


def _ffn_kernel(tm, halo, row_w, final, x_ref, xp_ref, xn_ref, mod_ref, ng_ref, wa_ref, wv_ref, cw_ref, cb_ref,
                wd_ref, fg_ref, o_ref, h_ref, a_ref, g_ref):
    i = pl.program_id(1)
    nt = pl.num_programs(1)
    sh = mod_ref[0, 3:4, :]
    sc = mod_ref[0, 4:5, :]
    g = ng_ref[...]
    x = x_ref[0]
    h_ref[halo:halo + tm, :] = _norm_mod(x, g, sc, sh).astype(BF16)
    if halo:
        h_ref[0:halo, :] = _norm_mod(xp_ref[0], g, sc, sh).astype(BF16)
        h_ref[halo + tm:, :] = _norm_mod(xn_ref[0], g, sc, sh).astype(BF16)
        keep_p = (i > 0).astype(F32)
        keep_n = (i < nt - 1).astype(F32)
    pad = SUBLANE
    zeros = jnp.zeros((pad, FT), F32)
    for slot in range(2):
        a_ref[slot, 0:pad, :] = zeros
        a_ref[slot, pad + tm + 2 * halo:, :] = zeros
    col = lax.broadcasted_iota(jnp.int32, (tm, FT), 0) % row_w
    has_left = col >= 1
    has_right = col <= row_w - 2
    dys = (-1, 0, 1) if halo else (0,)
    n_tiles = FFN_DIM // FT

    def up(f):
        a = _dot(h_ref[...], wa_ref[f])
        slot = f % 2
        if halo:
            a_ref[slot, pad:pad + halo, :] = a[0:halo] * keep_p
            a_ref[slot, pad + halo:pad + halo + tm, :] = a[halo:halo + tm]
            a_ref[slot, pad + halo + tm:pad + tm + 2 * halo, :] = a[halo + tm:] * keep_n
        else:
            a_ref[slot, pad:pad + tm, :] = a
        return _dot(h_ref[halo:halo + tm, :], wv_ref[f])

    val = up(0)
    for f in range(n_tiles):
        nxt_val = up(f + 1) if f + 1 < n_tiles else None
        slot = f % 2
        cw = cw_ref[f]
        cols = []
        for dx in (-1, 0, 1):
            s = None
            for dy in dys:
                tap = cw[(dy + 1) * FFN_CONV + dx + 1:(dy + 1) * FFN_CONV + dx + 2, :]
                term = tap * a_ref[slot, pl.ds(pad + halo + dy * row_w + dx, tm), :]
                s = term if s is None else s + term
            cols.append(s)
        conv = cols[1] + jnp.where(has_left, cols[0], 0.0) + jnp.where(has_right, cols[2], 0.0)
        g_ref[:, f * FT:(f + 1) * FT] = (_silu(conv + cb_ref[f]) * val).astype(BF16)
        val = nxt_val
    y = x + mod_ref[0, 5:6, :] * _dot(g_ref[...], wd_ref[...])
    if final:
        ms = jnp.mean(y * y, axis=-1, keepdims=True)
        y = y * lax.rsqrt(ms + EPS) * fg_ref[...]
    o_ref[0] = y


def _ffn(x, mod, per_batch, ng, w, latent, final_g=None):
    bsz, t_len, _ = x.shape
    tm = FFN_TM if latent else t_len
    halo = GRID_W if latent else 0
    row_w = GRID_W if latent else t_len
    hb = max(halo, SUBLANE)
    per = tm // hb
    last = t_len // hb - 1
    main = pl.BlockSpec((1, tm, D_MODEL), lambda b, i: (b, i, 0))
    prev = pl.BlockSpec((1, hb, D_MODEL), lambda b, i: (b, jnp.maximum(i * per - 1, 0), 0))
    nxt = pl.BlockSpec((1, hb, D_MODEL), lambda b, i: (b, jnp.minimum((i + 1) * per, last), 0))
    final = final_g is not None
    fg = (final_g if final else jnp.ones((D_MODEL,), F32)).reshape(1, D_MODEL)
    kern = functools.partial(_ffn_kernel, tm, halo, row_w, final)
    return pl.pallas_call(
        kern,
        grid=(bsz, t_len // tm),
        in_specs=[main, prev, nxt, _mod_spec(per_batch), _const_spec((1, D_MODEL)),
                  _const_spec(w['a'].shape), _const_spec(w['v'].shape), _const_spec(w['cw'].shape),
                  _const_spec(w['cb'].shape), _const_spec(w['d'].shape), _const_spec((1, D_MODEL))],
        out_specs=main,
        out_shape=jax.ShapeDtypeStruct(x.shape, F32),
        scratch_shapes=[pltpu.VMEM((tm + 2 * halo, D_MODEL), BF16),
                        pltpu.VMEM((2, tm + 2 * halo + 2 * SUBLANE, FT), F32),
                        pltpu.VMEM((tm, FFN_DIM), BF16)],
        compiler_params=_cp(("arbitrary", "arbitrary")),
    )(x, x, x, mod, ng.reshape(1, D_MODEL), w['a'], w['v'], w['cw'], w['cb'], w['d'], fg)


def _ffn_weights(w_up, conv_w, conv_b, w_down):
    nf = FFN_DIM // FT
    tiles = lambda m: m.reshape(D_MODEL, nf, FT).transpose(1, 0, 2).astype(BF16)
    return {'a': tiles(w_up[:, :FFN_DIM]), 'v': tiles(w_up[:, FFN_DIM:]),
            'cw': conv_w.reshape(FFN_CONV * FFN_CONV, nf, FT).transpose(1, 0, 2),
            'cb': conv_b.reshape(nf, 1, FT),
            'd': w_down.astype(BF16)}


def _c_in_kernel(x_ref, xp_ref, xn_ref, mod_ref, ng_ref, wg_ref, wx_ref, cw_ref, cb_ref, gact_ref, xc_ref,
                 ext_ref):
    i = pl.program_id(1)
    nt = pl.num_programs(1)
    sh = mod_ref[0, 0:1, :]
    sc = mod_ref[0, 1:2, :]
    g = ng_ref[...]
    hb = _norm_mod(x_ref[0], g, sc, sh).astype(BF16)
    hp = _norm_mod(xp_ref[0], g, sc, sh).astype(BF16)
    hn = _norm_mod(xn_ref[0], g, sc, sh).astype(BF16)
    gact_ref[0] = _gelu_tanh(_dot(hb, wg_ref[...]))
    ext_ref[0:CONV_HALO, :] = _dot(hp, wx_ref[...]) * (i > 0).astype(F32)
    ext_ref[CONV_HALO:CONV_HALO + TB, :] = _dot(hb, wx_ref[...])
    ext_ref[CONV_HALO + TB:, :] = _dot(hn, wx_ref[...]) * (i < nt - 1).astype(F32)
    xc_ref[0] = _conv_tokens(ext_ref, cw_ref, C_CONV) + cb_ref[...]


def _c_in(x, mod, per_batch, ng, wg, wx, conv_w, conv_b):
    bsz, t_len, _ = x.shape
    main, prev, nxt = _halo_specs(t_len, D_MODEL)
    tok = pl.BlockSpec((1, TB, C_WIDTH), lambda b, i: (b, i, 0))
    return pl.pallas_call(
        _c_in_kernel,
        grid=(bsz, t_len // TB),
        in_specs=[main, prev, nxt, _mod_spec(per_batch), _const_spec((1, D_MODEL)), _const_spec(wg.shape),
                  _const_spec(wx.shape), _const_spec(conv_w.shape), _const_spec((1, C_WIDTH))],
        out_specs=(tok, tok),
        out_shape=(jax.ShapeDtypeStruct((bsz, t_len, C_WIDTH), F32),) * 2,
        scratch_shapes=[pltpu.VMEM((TB + 2 * CONV_HALO, C_WIDTH), F32)],
        compiler_params=_cp(("arbitrary", "arbitrary")),
    )(x, x, x, mod, ng.reshape(1, D_MODEL), wg, wx, conv_w, conv_b.reshape(1, C_WIDTH))


def _rglru_scan_kernel(xc_ref, gw_ref, gb_ref, lam_ref, h0_ref, o_ref, hfin_ref, a_ref, u_ref, carry_ref):
    d = pl.program_id(1)
    i = pl.program_id(2)
    nt = pl.num_programs(2)

    @pl.when(i == 0)
    def _():
        carry_ref[...] = h0_ref[0, 0]

    for hh in range(C_HEADS):
        sl = slice(hh * C_BLOCK, (hh + 1) * C_BLOCK)
        xh = xc_ref[0, :, sl]
        gates = _sigmoid(_dot(xh.astype(BF16), gw_ref[0, hh]) + gb_ref[0, hh])
        r = gates[:, :C_BLOCK]
        gi = gates[:, C_BLOCK:]
        log_a = (-C_GATE_C) * r * _softplus(-lam_ref[0, :, sl])
        a_ref[:, sl] = jnp.exp(log_a)
        u_ref[:, sl] = jnp.sqrt(1.0 - jnp.exp(2.0 * log_a)) * (gi * xh)

    n_grp = TB // SUBLANE
    row = lax.broadcasted_iota(jnp.int32, (SUBLANE, C_WIDTH), 0)

    def scan_block(fwd):
        def group(n, carry):
            grp = n if fwd else n_grp - 1 - n
            start = pl.multiple_of(grp * SUBLANE, SUBLANE)
            a = a_ref[pl.ds(start, SUBLANE), :]
            u = u_ref[pl.ds(start, SUBLANE), :]
            s = 1
            while s < SUBLANE:
                has_prev = (row >= s) if fwd else (row < SUBLANE - s)
                shift = s if fwd else SUBLANE - s
                a_s = jnp.where(has_prev, pltpu.roll(a, shift, axis=0), 1.0)
                u_s = jnp.where(has_prev, pltpu.roll(u, shift, axis=0), 0.0)
                a, u = a * a_s, a * u_s + u
                s *= 2
            h = a * carry + u
            o_ref[0, 0, pl.ds(start, SUBLANE), :] = h
            return h[SUBLANE - 1:SUBLANE, :] if fwd else h[0:1, :]

        carry_ref[...] = lax.fori_loop(0, n_grp, group, carry_ref[...])

    @pl.when(d == 0)
    def _():
        scan_block(True)

    @pl.when(d == 1)
    def _():
        scan_block(False)

    @pl.when(i == nt - 1)
    def _():
        hfin_ref[0, 0] = carry_ref[...]


def _rglru_scan(xc, gate_w, gate_b, lam, h0):
    bsz, t_len, _ = xc.shape
    nt = t_len // TB
    st_spec = pl.BlockSpec((1, 1, 1, C_WIDTH), lambda b, d, i: (b, d, 0, 0))
    return pl.pallas_call(
        _rglru_scan_kernel,
        grid=(bsz, 2, nt),
        in_specs=[pl.BlockSpec((1, TB, C_WIDTH), lambda b, d, i: (b, _time_block(d, i, nt), 0)),
                  pl.BlockSpec((1,) + gate_w.shape[1:], lambda b, d, i: (d, 0, 0, 0)),
                  pl.BlockSpec((1,) + gate_b.shape[1:], lambda b, d, i: (d, 0, 0, 0)),
                  pl.BlockSpec((1, 1, C_WIDTH), lambda b, d, i: (d, 0, 0)),
                  st_spec],
        out_specs=(pl.BlockSpec((1, 1, TB, C_WIDTH), lambda b, d, i: (d, b, _time_block(d, i, nt), 0)), st_spec),
        out_shape=(jax.ShapeDtypeStruct((2, bsz, t_len, C_WIDTH), F32),
                   jax.ShapeDtypeStruct((bsz, 2, 1, C_WIDTH), F32)),
        scratch_shapes=[pltpu.VMEM((TB, C_WIDTH), F32), pltpu.VMEM((TB, C_WIDTH), F32),
                        pltpu.VMEM((1, C_WIDTH), F32)],
        compiler_params=_cp(("arbitrary", "arbitrary", "arbitrary")),
    )(xc, gate_w, gate_b, lam, h0)


def _c_out_kernel(x_ref, hf_ref, hb_ref, gact_ref, mod_ref, w_ref, o_ref):
    y = _dot((gact_ref[0] * (hf_ref[0, 0] + hb_ref[0, 0])).astype(BF16), w_ref[...])
    o_ref[0] = x_ref[0] + mod_ref[0, 2:3, :] * y


def _c_out(x, hdir, gact, mod, per_batch, w_out):
    bsz, t_len, _ = x.shape
    tok = pl.BlockSpec((1, TB, D_MODEL), lambda b, i: (b, i, 0))
    dir_spec = lambda d: pl.BlockSpec((1, 1, TB, C_WIDTH), lambda b, i: (d, b, i, 0))
    return pl.pallas_call(
        _c_out_kernel,
        grid=(bsz, t_len // TB),
        in_specs=[tok, dir_spec(0), dir_spec(1), tok, _mod_spec(per_batch), _const_spec(w_out.shape)],
        out_specs=tok,
        out_shape=jax.ShapeDtypeStruct(x.shape, F32),
        compiler_params=_cp(("arbitrary", "arbitrary")),
    )(x, hdir, hdir, gact, mod, w_out)


def _mod_rows(mods_l, bsz):
    m = mods_l.reshape(mods_l.shape[0], 6, D_MODEL)
    m = jnp.pad(m, ((0, 0), (0, SUBLANE - 6), (0, 0)))
    return m[:bsz], m[bsz:bsz + 1]


def kernel(x, c, ctx, c_ctx, mod_w, mod_b, norm_mix_g, norm_ffn_g, final_norm_g, ab_w_in, hgrn_lb, hgrn_norm_g, gdn_conv_w, gdn_a_log, gdn_dt_bias, gdn_norm_g, ab_w_out, c_w_in, c_conv_w, c_conv_b, c_gate_w, c_gate_b, c_lambda, c_w_out, ffn_w_up, ffn_conv_w, ffn_conv_b, ffn_w_down):
    bsz = x.shape[0]
    rows = 2 * SUBLANE
    cc = jnp.concatenate([c, c_ctx[None, :], jnp.zeros((rows - bsz - 1, D_MODEL), F32)], axis=0)
    mods = _modulation(cc, mod_w, mod_b)
    xc = ctx
    for l in range(DEPTH):
        last = l == DEPTH - 1
        mod_x, mod_c = _mod_rows(mods[l], bsz)
        ffn_w = _ffn_weights(ffn_w_up[l], ffn_conv_w[l], ffn_conv_b[l], ffn_w_down[l])
        if l % 2 == 0:
            e = l // 2
            w_in = ab_w_in[e]
            o = 5 * HD
            w = {'a': w_in[:, :o].astype(BF16), 'qkv': w_in[:, o:o + QKV].astype(BF16),
                 'z': w_in[:, o + QKV:o + QKV + HD].astype(BF16),
                 'alpha': w_in[:, o + QKV + HD:o + QKV + HD + 2 * B_HEADS],
                 'beta': w_in[:, o + QKV + HD + 2 * B_HEADS:]}
            w_out = ab_w_out[e].astype(BF16)
            feats = []
            for seq, mod, per_batch in ((xc, mod_c, False), (x, mod_x, True)):
                feats.append(_ab_in(l, seq, mod, per_batch, norm_mix_g[l], w, hgrn_lb, gdn_conv_w[e],
                                    gdn_a_log[e], gdn_dt_bias[e]))
            outs = []
            sa = jnp.zeros((bsz, 2, A_HEADS, A_DV, A_DK), F32)
            sb = jnp.zeros((bsz, 2, B_HEADS, B_DV, B_DK), F32)
            for qa, ka, lfa, va, ga, qb, kb, vb, zb, gbc, gbr in feats:
                oa, sa = _hgrn_scan(qa, ka, lfa, va, sa)
                ob, sb = _gdn_scan(qb, kb, vb, gbc, gbr, sb)
                outs.append((oa, ob, ga, zb))
            oa, ob, ga, zb = outs[1]
            x = _ab_out(x, oa, ob, ga, zb, mod_x, True, hgrn_norm_g[e], gdn_norm_g[e], w_out)
            if not last:
                oa, ob, ga, zb = outs[0]
                xc = _ab_out(xc, oa, ob, ga, zb, mod_c, False, hgrn_norm_g[e], gdn_norm_g[e], w_out)
        else:
            o = l // 2
            wg = c_w_in[o][:, :C_WIDTH].astype(BF16)
            wx = c_w_in[o][:, C_WIDTH:].astype(BF16)
            gate_w = c_gate_w[o].astype(BF16)
            gate_b = c_gate_b[o].reshape(2, C_HEADS, 1, 2 * C_BLOCK)
            lam = c_lambda[o].reshape(2, 1, C_WIDTH)
            h_state = jnp.zeros((bsz, 2, 1, C_WIDTH), F32)
            res = []
            for seq, mod, per_batch in ((xc, mod_c, False), (x, mod_x, True)):
                gact, xconv = _c_in(seq, mod, per_batch, norm_mix_g[l], wg, wx, c_conv_w[o], c_conv_b[o])
                hdir, h_state = _rglru_scan(xconv, gate_w, gate_b, lam, h_state)
                res.append((hdir, gact))
            x = _c_out(x, res[1][0], res[1][1], mod_x, True, c_w_out[o].astype(BF16))
            if not last:
                xc = _c_out(xc, res[0][0], res[0][1], mod_c, False, c_w_out[o].astype(BF16))
        if not last:
            xc = _ffn(xc, mod_c, False, norm_ffn_g[l], ffn_w, latent=False)
        x = _ffn(x, mod_x, True, norm_ffn_g[l], ffn_w, latent=True, final_g=final_norm_g if last else None)
    return x
```

```python
import functools
import math

import numpy as np
import jax
import jax.numpy as jnp
from jax import lax
from jax.experimental import pallas as pl
from jax.experimental.pallas import tpu as pltpu

F32 = jnp.float32
BF16 = jnp.bfloat16

D_MODEL = 1024
DEPTH = 2
GRID_W = 64
EPS = 1e-6
A_HEADS = 4
A_DK = 128
A_DV = 128
B_HEADS = 4
B_DK = 128
B_DV = 128
B_CONV = 4
C_WIDTH = D_MODEL
C_HEADS = 4
C_BLOCK = C_WIDTH // C_HEADS
C_CONV = 4
C_GATE_C = 8.0
FFN_DIM = 2816
FFN_CONV = 3

HD = A_HEADS * A_DK
QKV = 2 * B_HEADS * B_DK + B_HEADS * B_DV

LANE = 128
SUBLANE = 8
VMEM_LIMIT = 56 * 1024 * 1024

TB = 256
SUB = 16
LEVELS = (16, 32, 64, 128)
FT = 256
FFN_TM = 1024
CONV_HALO = 8


def _cp(sem):
    return pltpu.CompilerParams(dimension_semantics=sem, vmem_limit_bytes=VMEM_LIMIT)


def _dot(a, b):
    return jnp.dot(a, b, preferred_element_type=F32)


def _dot_nt(a, b):
    return lax.dot_general(a, b, (((1,), (1,)), ((), ())), preferred_element_type=F32)


def _dot_tn(a, b):
    return lax.dot_general(a, b, (((0,), (0,)), ((), ())), preferred_element_type=F32)


def _sigmoid(x):
    return 0.5 * jnp.tanh(0.5 * x) + 0.5


def _silu(x):
    return x * _sigmoid(x)


def _softplus(x):
    return jnp.maximum(x, 0.0) + jnp.log1p(jnp.exp(-jnp.abs(x)))


def _gelu_tanh(x):
    c = math.sqrt(2.0 / math.pi)
    return 0.5 * x * (1.0 + jnp.tanh(c * (x + 0.044715 * (x * x * x))))


def _hi_lo(x):
    hi = x.astype(BF16)
    lo = (x - hi.astype(F32)).astype(BF16)
    return hi, lo


def _norm_mod(x, g, sc, sh):
    ms = jnp.mean(x * x, axis=-1, keepdims=True)
    return (x * lax.rsqrt(ms + EPS)) * g * (1.0 + sc) + sh


def _const_spec(shape):
    nd = len(shape)
    return pl.BlockSpec(shape, lambda *_: (0,) * nd, pipeline_mode=pl.Buffered(1))


MOD_TN = 1024


def _mod_kernel(c_ref, w_ref, b_ref, o_ref):
    s = _silu(c_ref[...])
    w = w_ref[0]
    s_hi, s_lo = _hi_lo(s)
    w_hi, w_lo = _hi_lo(w)
    acc = _dot(s_hi, w_hi) + _dot(s_hi, w_lo) + _dot(s_lo, w_hi)
    o_ref[0] = acc + b_ref[0]


def _modulation(cc, mod_w, mod_b):
    rows = cc.shape[0]
    n = mod_w.shape[-1]
    return pl.pallas_call(
        _mod_kernel,
        grid=(DEPTH, n // MOD_TN),
        in_specs=[pl.BlockSpec((rows, D_MODEL), lambda l, j: (0, 0)),
                  pl.BlockSpec((1, D_MODEL, MOD_TN), lambda l, j: (l, 0, j)),
                  pl.BlockSpec((1, 1, MOD_TN), lambda l, j: (l, 0, j))],
        out_specs=pl.BlockSpec((1, rows, MOD_TN), lambda l, j: (l, 0, j)),
        out_shape=jax.ShapeDtypeStruct((DEPTH, rows, n), F32),
        compiler_params=_cp(("arbitrary", "arbitrary")),
    )(cc, mod_w, mod_b.reshape(DEPTH, 1, n))


def _halo_specs(t_len, width):
    per = TB // CONV_HALO
    last = t_len // CONV_HALO - 1
    main = pl.BlockSpec((1, TB, width), lambda b, i: (b, i, 0))
    prev = pl.BlockSpec((1, CONV_HALO, width), lambda b, i: (b, jnp.maximum(i * per - 1, 0), 0))
    nxt = pl.BlockSpec((1, CONV_HALO, width), lambda b, i: (b, jnp.minimum((i + 1) * per, last), 0))
    return main, prev, nxt


def _mod_spec(per_batch):
    if per_batch:
        return pl.BlockSpec((1, SUBLANE, D_MODEL), lambda b, *_: (b, 0, 0))
    return pl.BlockSpec((1, SUBLANE, D_MODEL), lambda b, *_: (0, 0, 0))


def _conv_tokens(ext_ref, w_ref, width):
    left = width // 2
    acc = None
    for j in range(width):
        term = w_ref[j:j + 1, :] * ext_ref[pl.ds(CONV_HALO + j - left, TB), :]
        acc = term if acc is None else acc + term
    return acc


def _ab_in_kernel(layer, x_ref, xp_ref, xn_ref, mod_ref, ng_ref, wa_ref, wqkv_ref, wz_ref, wab_ref, wabt_ref,
                  lbp_ref, convw_ref, coef_ref, coeft_ref,
                  qa_ref, ka_ref, lfa_ref, va_ref, ga_ref, qb_ref, kb_ref, vb_ref, zb_ref, gbc_ref, gbr_ref,
                  ext_ref):
    i = pl.program_id(1)
    nt = pl.num_programs(1)
    sh = mod_ref[0, 0:1, :]
    sc = mod_ref[0, 1:2, :]
    g = ng_ref[...]
    h_all = jnp.concatenate([_norm_mod(x_ref[0], g, sc, sh), _norm_mod(xp_ref[0], g, sc, sh),
                             _norm_mod(xn_ref[0], g, sc, sh)], axis=0).astype(BF16)
    hb = h_all[:TB]
    keep_p = (i > 0).astype(F32)
    keep_n = (i < nt - 1).astype(F32)

    def lower_bound(d):
        rws = [lbp_ref[d * (DEPTH + 1) + r:d * (DEPTH + 1) + r + 1, :] for r in range(DEPTH + 1)]
        mx = functools.reduce(jnp.maximum, rws)
        es = [jnp.exp(r - mx) for r in rws]
        return functools.reduce(jnp.add, es[:layer + 1]) / functools.reduce(jnp.add, es)

    def a_proj(group):
        return lambda: _dot(hb, wa_ref[:, group * HD:(group + 1) * HD])

    def put_q(p):
        qa_ref[0] = _silu(p) * (A_DK ** -0.5)

    def put_forget(d):
        def put(p):
            lb_d = lower_bound(d)
            ka_ref[d, 0] = (1.0 - lb_d) * _sigmoid(-p)
            lfa_ref[d, 0] = jnp.log(lb_d + (1.0 - lb_d) * _sigmoid(p))
        return put

    def put_v(p):
        va_ref[0] = p

    def put_gate(p):
        ga_ref[0] = _silu(p)

    def conv_proj(part):
        def run():
            proj = _dot(h_all, wqkv_ref[:, part * HD:(part + 1) * HD])
            slot = part % 2
            ext_ref[slot, 0:CONV_HALO, :] = proj[TB:TB + CONV_HALO] * keep_p
            ext_ref[slot, CONV_HALO:CONV_HALO + TB, :] = proj[:TB]
            ext_ref[slot, CONV_HALO + TB:, :] = proj[TB + CONV_HALO:] * keep_n
        return run

    def put_conv(part):
        def put(_):
            lo = part * HD
            t_all = _silu(_conv_tokens(ext_ref.at[part % 2], convw_ref.at[:, lo:lo + HD], B_CONV))
            if part == 2:
                vb_ref[0] = t_all
                return
            ref, scale = ((qb_ref, B_DK ** -0.5), (kb_ref, 1.0))[part]
            for hh in range(B_HEADS):
                t = t_all[:, hh * B_DK:(hh + 1) * B_DK]
                ss = jnp.sum(t * t, axis=-1, keepdims=True)
                ref[0, :, hh * B_DK:(hh + 1) * B_DK] = t * (lax.rsqrt(ss + EPS) * scale)
        return put

    def put_z(p):
        zb_ref[0] = _silu(p)

    def gate_proj():
        return _dot(hb, wab_ref[...]), _dot_nt(wabt_ref[...], hb)

    def put_gates(p):
        ab, abt = p
        lane = lax.broadcasted_iota(jnp.int32, ab.shape, 1)
        is_g = (lane % SUBLANE) < B_HEADS
        act = jnp.where(is_g, -jnp.exp(coef_ref[0:1, :]) * _softplus(ab + coef_ref[1:2, :]), _sigmoid(ab))
        row = lax.broadcasted_iota(jnp.int32, abt.shape, 0)
        is_g_t = (row % SUBLANE) < B_HEADS
        act_t = jnp.where(is_g_t, -jnp.exp(coeft_ref[:, 0:1]) * _softplus(abt + coeft_ref[:, 1:2]), _sigmoid(abt))
        col = lax.broadcasted_iota(jnp.int32, (TB, LANE), 1)
        for d in range(2):
            shifted = act if d == 0 else pltpu.roll(act, LANE - SUBLANE, axis=1)
            gbc_ref[0, d] = jnp.where(col < SUBLANE, shifted, 0.0)
            gbr_ref[0, d] = act_t[d * SUBLANE:(d + 1) * SUBLANE, :]

    stages = [(a_proj(0), put_q), (a_proj(1), put_forget(0)), (a_proj(2), put_forget(1)), (a_proj(3), put_v),
              (a_proj(4), put_gate), (conv_proj(0), put_conv(0)), (conv_proj(1), put_conv(1)),
              (conv_proj(2), put_conv(2)), (lambda: _dot(hb, wz_ref[...]), put_z), (gate_proj, put_gates)]
    pending = stages[0][0]()
    for s, (_, consume) in enumerate(stages):
        nxt = stages[s + 1][0]() if s + 1 < len(stages) else None
        consume(pending)
        pending = nxt


def _ab_in(layer, x, mod, per_batch, ng, w, hgrn_lb, conv_w, a_log, dt_bias):
    bsz, t_len, _ = x.shape
    nt = t_len // TB
    main, prev, nxt = _halo_specs(t_len, D_MODEL)
    tok = lambda width: pl.BlockSpec((1, TB, width), lambda b, i: (b, i, 0))
    tok2 = lambda width: pl.BlockSpec((2, 1, TB, width), lambda b, i: (0, b, i, 0))
    sds = lambda *s: jax.ShapeDtypeStruct(s, F32)
    out_shape = (sds(bsz, t_len, HD), sds(2, bsz, t_len, HD), sds(2, bsz, t_len, HD), sds(bsz, t_len, HD),
                 sds(bsz, t_len, HD), sds(bsz, t_len, HD), sds(bsz, t_len, HD), sds(bsz, t_len, HD),
                 sds(bsz, t_len, HD), sds(bsz, 2, t_len, LANE), sds(bsz, 2, SUBLANE, t_len))
    out_specs = (tok(HD), tok2(HD), tok2(HD), tok(HD), tok(HD), tok(HD), tok(HD), tok(HD), tok(HD),
                 pl.BlockSpec((1, 2, TB, LANE), lambda b, i: (b, 0, i, 0)),
                 pl.BlockSpec((1, 2, SUBLANE, TB), lambda b, i: (b, 0, 0, i)))
    w_al, w_be = w['alpha'], w['beta']
    cols = [w_al[:, 0:4], w_be[:, 0:4], w_al[:, 4:8], w_be[:, 4:8]]
    wab = jnp.concatenate(cols, axis=1)
    wab_p = jnp.pad(wab, ((0, 0), (0, LANE - 2 * SUBLANE))).astype(BF16)
    wabt = wab.T.astype(BF16)
    zero4 = jnp.zeros((B_HEADS,), F32)
    a_flat = jnp.concatenate([a_log[0], zero4, a_log[1], zero4])
    d_flat = jnp.concatenate([dt_bias[0], zero4, dt_bias[1], zero4])
    coef = jnp.pad(jnp.stack([a_flat, d_flat]), ((0, 0), (0, LANE - 2 * SUBLANE)))
    coeft = jnp.stack([a_flat, d_flat], axis=1)
    hgrn_lb = hgrn_lb.reshape(2 * (DEPTH + 1), HD)
    args = (x, x, x, mod, ng.reshape(1, D_MODEL), w['a'], w['qkv'], w['z'], wab_p, wabt, hgrn_lb, conv_w, coef,
            coeft)
    in_specs = [main, prev, nxt, _mod_spec(per_batch), _const_spec((1, D_MODEL)), _const_spec(w['a'].shape),
                _const_spec(w['qkv'].shape), _const_spec(w['z'].shape), _const_spec(wab_p.shape),
                _const_spec(wabt.shape), _const_spec(hgrn_lb.shape), _const_spec(conv_w.shape),
                _const_spec(coef.shape), _const_spec(coeft.shape)]
    return pl.pallas_call(
        functools.partial(_ab_in_kernel, layer),
        grid=(bsz, nt),
        in_specs=in_specs,
        out_specs=out_specs,
        out_shape=out_shape,
        scratch_shapes=[pltpu.VMEM((2, TB + 2 * CONV_HALO, HD), F32)],
        compiler_params=_cp(("arbitrary", "arbitrary")),
    )(*args)


def _both_dirs(stack):
    return np.stack([stack, stack[:, ::-1, ::-1]])


@functools.lru_cache(maxsize=None)
def _hgrn_consts():
    t = np.arange(TB)
    i, j = t[:, None], t[None, :]
    sums = [(i // SUB == j // SUB) & (j <= i)]
    masks = [(i // SUB == j // SUB) & (j <= i)]
    for m in LEVELS:
        seg = 2 * m
        same = (i // seg) == (j // seg)
        mid = (i // seg) * seg + m - 1
        second = (i % seg) >= m
        sums.append(same & np.where(second, (j > mid) & (j <= i), (j > i) & (j <= mid)))
        masks.append(same & second & ((j % seg) < m))
    sums.append(j <= i)
    sums.append(j > i)
    sums = _both_dirs(np.stack(sums).astype(np.float32))
    masks = _both_dirs(np.stack(masks).astype(np.float32))
    n = sums.shape[1]
    return jnp.asarray(sums.reshape(2, n * TB, TB), BF16), jnp.asarray(masks, F32)


@functools.lru_cache(maxsize=None)
def _gdn_consts():
    t = np.arange(TB)
    i, j = t[:, None], t[None, :]
    tri = _both_dirs(np.stack([j <= i, j < i]).astype(np.float32))
    blocks = [(i // SUB) == (j // SUB)]
    for m in LEVELS:
        seg = 2 * m
        blocks.append(((i // seg) == (j // seg)) & ((i // m) != (j // m)))
    blocks = np.stack(blocks).astype(np.float32)
    return jnp.asarray(tri[:, 0], BF16), jnp.asarray(tri, F32), jnp.asarray(blocks, F32)


def _time_block(d, i, nt):
    return i + d * (nt - 1 - 2 * i)


def _hgrn_scan_kernel(q_ref, k_ref, lf_ref, v_ref, sums_ref, masks_ref, s0_ref, o_ref, sfin_ref, st_ref):
    i = pl.program_id(2)
    nt = pl.num_programs(2)

    @pl.when(i == 0)
    def _():
        st_ref[...] = s0_ref[0, 0]

    n_lv = len(LEVELS)
    heads = range(A_HEADS)
    sls = [slice(hh * A_DK, (hh + 1) * A_DK) for hh in heads]
    xxs = []
    for hh in heads:
        lf_hi, lf_lo = _hi_lo(lf_ref[0, 0, :, sls[hh]])
        xx = _dot(sums_ref[0], jnp.concatenate([lf_hi, lf_lo], axis=1))
        xxs.append(xx[:, :A_DK] + xx[:, A_DK:])
    attns = []
    for hh in heads:
        x0 = xxs[hh][0:TB]
        raw = _dot_nt((q_ref[0, :, sls[hh]] * jnp.exp(x0)).astype(BF16),
                      (k_ref[0, 0, :, sls[hh]] * jnp.exp(-x0)).astype(BF16))
        attns.append(jnp.where(masks_ref[0, 0] > 0.0, raw, 0.0))
    for lv in range(1, n_lv + 1):
        for hh in heads:
            e = jnp.exp(xxs[hh][lv * TB:(lv + 1) * TB])
            attns[hh] = attns[hh] + masks_ref[0, lv] * _dot_nt((q_ref[0, :, sls[hh]] * e).astype(BF16),
                                                               (k_ref[0, 0, :, sls[hh]] * e).astype(BF16))
    sts = [st_ref[hh] for hh in heads]
    for hh in heads:
        e_in = jnp.exp(xxs[hh][(n_lv + 1) * TB:(n_lv + 2) * TB])
        o_ref[0, 0, :, sls[hh]] = (_dot(attns[hh].astype(BF16), v_ref[0, :, sls[hh]].astype(BF16))
                                   + _dot_nt((q_ref[0, :, sls[hh]] * e_in).astype(BF16), sts[hh].astype(BF16)))
    for hh in heads:
        e_out = jnp.exp(xxs[hh][(n_lv + 2) * TB:(n_lv + 3) * TB])
        gl = jnp.exp(jnp.sum(lf_ref[0, 0, :, sls[hh]], axis=0, keepdims=True))
        st_ref[hh] = sts[hh] * gl + _dot_tn(v_ref[0, :, sls[hh]].astype(BF16),
                                            (k_ref[0, 0, :, sls[hh]] * e_out).astype(BF16))

    @pl.when(i == nt - 1)
    def _():
        sfin_ref[0, 0] = st_ref[...]


def _hgrn_scan(q, k2, lf2, v, s0):
    bsz, t_len, _ = q.shape
    nt = t_len // TB
    sums, masks = _hgrn_consts()
    tok = pl.BlockSpec((1, TB, HD), lambda b, d, i: (b, _time_block(d, i, nt), 0))
    tok2 = pl.BlockSpec((1, 1, TB, HD), lambda b, d, i: (d, b, _time_block(d, i, nt), 0))
    st_spec = pl.BlockSpec((1, 1, A_HEADS, A_DV, A_DK), lambda b, d, i: (b, d, 0, 0, 0))
    return pl.pallas_call(
        _hgrn_scan_kernel,
        grid=(bsz, 2, nt),
        in_specs=[tok, tok2, tok2, tok,
                  pl.BlockSpec((1,) + sums.shape[1:], lambda b, d, i: (d, 0, 0)),
                  pl.BlockSpec((1,) + masks.shape[1:], lambda b, d, i: (d, 0, 0, 0)),
                  st_spec],
        out_specs=(tok2, st_spec),
        out_shape=(jax.ShapeDtypeStruct((2, bsz, t_len, HD), F32),
                   jax.ShapeDtypeStruct((bsz, 2, A_HEADS, A_DV, A_DK), F32)),
        scratch_shapes=[pltpu.VMEM((A_HEADS, A_DV, A_DK), F32)],
        compiler_params=_cp(("arbitrary", "arbitrary", "arbitrary")),
    )(q, k2, lf2, v, sums, masks, s0)


def _unit_tri_inverse_minus_identity(a_list, blocks_ref):
    b16 = lambda x: x.astype(BF16)
    ps = [a * blocks_ref[0] for a in a_list]
    es = [-p for p in ps]
    width = 1
    while 2 * width < SUB:
        ps = [_dot(b16(p), b16(p)) for p in ps]
        es = [e + p + _dot(b16(e), b16(p)) for e, p in zip(es, ps)]
        width *= 2
    for lv in range(len(LEVELS)):
        bms = [a * blocks_ref[lv + 1] for a in a_list]
        ys = [bm + _dot(b16(e), b16(bm)) for e, bm in zip(es, bms)]
        es = [e - (y + _dot(b16(y), b16(e))) for e, y in zip(es, ys)]
    return es


def _gdn_scan_kernel(q_ref, k_ref, v_ref, gbc_ref, gbr_ref, linc_ref, tri_ref, blocks_ref, s0_ref,
                     o_ref, sfin_ref, st_ref):
    i = pl.program_id(2)
    nt = pl.num_programs(2)

    @pl.when(i == 0)
    def _():
        st_ref[...] = s0_ref[0, 0]

    gcol = gbc_ref[0, 0]
    grow = gbr_ref[0, 0]
    linc = linc_ref[0]
    c_hi, c_lo = _hi_lo(gcol)
    gc_col = _dot(linc, jnp.concatenate([c_hi, c_lo], axis=1))
    gc_col = gc_col[:, :LANE] + gc_col[:, LANE:]
    r_hi, r_lo = _hi_lo(grow)
    gc_row = _dot_nt(jnp.concatenate([r_hi, r_lo], axis=0), linc)
    gc_row = gc_row[:SUBLANE] + gc_row[SUBLANE:]
    g_tot = jnp.sum(gcol, axis=0, keepdims=True)
    incl = tri_ref[0, 0] > 0.0
    strict = tri_ref[0, 1]

    heads = range(B_HEADS)
    sls = [slice(hh * B_DK, (hh + 1) * B_DK) for hh in heads]
    gccs = [gc_col[:, hh:hh + 1] for hh in heads]
    tots = [g_tot[:, hh:hh + 1] for hh in heads]
    betas = [gcol[:, B_HEADS + hh:B_HEADS + hh + 1] for hh in heads]
    a_list, attns, kbetas = [], [], []
    for hh in heads:
        q = q_ref[0, :, sls[hh]]
        k = k_ref[0, :, sls[hh]]
        gcr = gc_row[hh:hh + 1, :]
        decay = jnp.where(incl, jnp.exp(jnp.where(incl, gccs[hh] - gcr, 0.0)), 0.0)
        kbeta = k * betas[hh]
        prod = _dot_nt(jnp.concatenate([kbeta, q], axis=0).astype(BF16), k.astype(BF16))
        a_list.append(strict * prod[:TB] * decay)
        attns.append(prod[TB:] * decay)
        kbetas.append(kbeta)
    es = _unit_tri_inverse_minus_identity(a_list, blocks_ref)
    egs = [jnp.exp(gccs[hh]) for hh in heads]
    sols = []
    for hh in heads:
        rhs = jnp.concatenate([v_ref[0, :, sls[hh]] * betas[hh], kbetas[hh] * egs[hh]], axis=1)
        sols.append(rhs + _dot(es[hh].astype(BF16), rhs.astype(BF16)))
    sts = [st_ref[hh] for hh in heads]
    ws_qs = [_dot_nt(jnp.concatenate([sols[hh][:, B_DV:], q_ref[0, :, sls[hh]] * egs[hh]], axis=0).astype(BF16),
                     sts[hh].astype(BF16)) for hh in heads]
    v_news = [sols[hh][:, :B_DV] - ws_qs[hh][:TB] for hh in heads]
    for hh in heads:
        o_ref[0, 0, :, sls[hh]] = ws_qs[hh][TB:] + _dot(attns[hh].astype(BF16), v_news[hh].astype(BF16))
    for hh in heads:
        kd = k_ref[0, :, sls[hh]] * jnp.exp(tots[hh] - gccs[hh])
        st_ref[hh] = sts[hh] * jnp.exp(tots[hh]) + _dot_tn(v_news[hh].astype(BF16), kd.astype(BF16))

    @pl.when(i == nt - 1)
    def _():
        sfin_ref[0, 0] = st_ref[...]


def _gdn_scan(q, k, v, gbc, gbr, s0):
    bsz, t_len, _ = q.shape
    nt = t_len // TB
    linc, tri, blocks = _gdn_consts()
    tok = pl.BlockSpec((1, TB, HD), lambda b, d, i: (b, _time_block(d, i, nt), 0))
    tok2 = pl.BlockSpec((1, 1, TB, HD), lambda b, d, i: (d, b, _time_block(d, i, nt), 0))
    st_spec = pl.BlockSpec((1, 1, B_HEADS, B_DV, B_DK), lambda b, d, i: (b, d, 0, 0, 0))
    return pl.pallas_call(
        _gdn_scan_kernel,
        grid=(bsz, 2, nt),
        in_specs=[tok, tok, tok,
                  pl.BlockSpec((1, 1, TB, LANE), lambda b, d, i: (b, d, _time_block(d, i, nt), 0)),
                  pl.BlockSpec((1, 1, SUBLANE, TB), lambda b, d, i: (b, d, 0, _time_block(d, i, nt))),
                  pl.BlockSpec((1, TB, TB), lambda b, d, i: (d, 0, 0)),
                  pl.BlockSpec((1, 2, TB, TB), lambda b, d, i: (d, 0, 0, 0)),
                  _const_spec(blocks.shape),
                  st_spec],
        out_specs=(tok2, st_spec),
        out_shape=(jax.ShapeDtypeStruct((2, bsz, t_len, HD), F32),
                   jax.ShapeDtypeStruct((bsz, 2, B_HEADS, B_DV, B_DK), F32)),
        scratch_shapes=[pltpu.VMEM((B_HEADS, B_DV, B_DK), F32)],
        compiler_params=_cp(("arbitrary", "arbitrary", "arbitrary")),
    )(q, k, v, gbc, gbr, linc, tri, blocks, s0)


def _head_norm_gate(o, g, gate, heads, width):
    parts = []
    for hh in range(heads):
        t = o[:, hh * width:(hh + 1) * width]
        ms = jnp.mean(t * t, axis=-1, keepdims=True)
        parts.append(t * lax.rsqrt(ms + EPS) * g)
    return jnp.concatenate(parts, axis=1) * gate


def _ab_out_kernel(x_ref, oaf_ref, oab_ref, obf_ref, obb_ref, ga_ref, zb_ref, mod_ref, ag_ref, bg_ref, w_ref,
                   o_ref):
    ya = _head_norm_gate(oaf_ref[0, 0] + oab_ref[0, 0], ag_ref[...], ga_ref[0], A_HEADS, A_DV)
    yb = _head_norm_gate(obf_ref[0, 0] + obb_ref[0, 0], bg_ref[...], zb_ref[0], B_HEADS, B_DV)
    y = _dot(jnp.concatenate([ya, yb], axis=1).astype(BF16), w_ref[...])
    o_ref[0] = x_ref[0] + mod_ref[0, 2:3, :] * y


def _ab_out(x, oa, ob, ga, zb, mod, per_batch, hgrn_g, gdn_g, w_out):
    bsz, t_len, _ = x.shape
    tok = lambda width: pl.BlockSpec((1, TB, width), lambda b, i: (b, i, 0))
    dir_spec = lambda d: pl.BlockSpec((1, 1, TB, HD), lambda b, i: (d, b, i, 0))
    return pl.pallas_call(
        _ab_out_kernel,
        grid=(bsz, t_len // TB),
        in_specs=[tok(D_MODEL), dir_spec(0), dir_spec(1), dir_spec(0), dir_spec(1), tok(HD), tok(HD),
                  _mod_spec(per_batch), _const_spec((1, A_DV)), _const_spec((1, B_DV)),
                  _const_spec(w_out.shape)],
        out_specs=tok(D_MODEL),
        out_shape=jax.ShapeDtypeStruct(x.shape, F32),
        compiler_params=_cp(("arbitrary", "arbitrary")),
    )(x, oa, oa, ob, ob, ga, zb, mod, hgrn_g.reshape(1, A_DV), gdn_g.reshape(1, B_DV), w_out)


def _ffn_kernel(tm, halo, row_w, final, x_ref, xp_ref, xn_ref, mod_ref, ng_ref, wa_ref, wv_ref, cw_ref, cb_ref,
                wd_ref, fg_ref, o_ref, h_ref, a_ref, g_ref):
    i = pl.program_id(1)
    nt = pl.num_programs(1)
    sh = mod_ref[0, 3:4, :]
    sc = mod_ref[0, 4:5, :]
    g = ng_ref[...]
    x = x_ref[0]
    h_ref[halo:halo + tm, :] = _norm_mod(x, g, sc, sh).astype(BF16)
    if halo:
        h_ref[0:halo, :] = _norm_mod(xp_ref[0], g, sc, sh).astype(BF16)
        h_ref[halo + tm:, :] = _norm_mod(xn_ref[0], g, sc, sh).astype(BF16)
        keep_p = (i > 0).astype(F32)
        keep_n = (i < nt - 1).astype(F32)
    pad = SUBLANE
    zeros = jnp.zeros((pad, FT), F32)
    for slot in range(2):
        a_ref[slot, 0:pad, :] = zeros
        a_ref[slot, pad + tm + 2 * halo:, :] = zeros
    col = lax.broadcasted_iota(jnp.int32, (tm, FT), 0) % row_w
    has_left = col >= 1
    has_right = col <= row_w - 2
    dys = (-1, 0, 1) if halo else (0,)
    n_tiles = FFN_DIM // FT

    def up(f):
        a = _dot(h_ref[...], wa_ref[f])
        slot = f % 2
        if halo:
            a_ref[slot, pad:pad + halo, :] = a[0:halo] * keep_p
            a_ref[slot, pad + halo:pad + halo + tm, :] = a[halo:halo + tm]
            a_ref[slot, pad + halo + tm:pad + tm + 2 * halo, :] = a[halo + tm:] * keep_n
        else:
            a_ref[slot, pad:pad + tm, :] = a
        return _dot(h_ref[halo:halo + tm, :], wv_ref[f])

    val = up(0)
    for f in range(n_tiles):
        nxt_val = up(f + 1) if f + 1 < n_tiles else None
        slot = f % 2
        cw = cw_ref[f]
        cols = []
        for dx in (-1, 0, 1):
            s = None
            for dy in dys:
                tap = cw[(dy + 1) * FFN_CONV + dx + 1:(dy + 1) * FFN_CONV + dx + 2, :]
                term = tap * a_ref[slot, pl.ds(pad + halo + dy * row_w + dx, tm), :]
                s = term if s is None else s + term
            cols.append(s)
        conv = cols[1] + jnp.where(has_left, cols[0], 0.0) + jnp.where(has_right, cols[2], 0.0)
        g_ref[:, f * FT:(f + 1) * FT] = (_silu(conv + cb_ref[f]) * val).astype(BF16)
        val = nxt_val
    y = x + mod_ref[0, 5:6, :] * _dot(g_ref[...], wd_ref[...])
    if final:
        ms = jnp.mean(y * y, axis=-1, keepdims=True)
        y = y * lax.rsqrt(ms + EPS) * fg_ref[...]
    o_ref[0] = y


def _ffn(x, mod, per_batch, ng, w, latent, final_g=None):
    bsz, t_len, _ = x.shape
    tm = FFN_TM if latent else t_len
    halo = GRID_W if latent else 0
    row_w = GRID_W if latent else t_len
    hb = max(halo, SUBLANE)
    per = tm // hb
    last = t_len // hb - 1
    main = pl.BlockSpec((1, tm, D_MODEL), lambda b, i: (b, i, 0))
    prev = pl.BlockSpec((1, hb, D_MODEL), lambda b, i: (b, jnp.maximum(i * per - 1, 0), 0))
    nxt = pl.BlockSpec((1, hb, D_MODEL), lambda b, i: (b, jnp.minimum((i + 1) * per, last), 0))
    final = final_g is not None
    fg = (final_g if final else jnp.ones((D_MODEL,), F32)).reshape(1, D_MODEL)
    kern = functools.partial(_ffn_kernel, tm, halo, row_w, final)
    return pl.pallas_call(
        kern,
        grid=(bsz, t_len // tm),
        in_specs=[main, prev, nxt, _mod_spec(per_batch), _const_spec((1, D_MODEL)),
                  _const_spec(w['a'].shape), _const_spec(w['v'].shape), _const_spec(w['cw'].shape),
                  _const_spec(w['cb'].shape), _const_spec(w['d'].shape), _const_spec((1, D_MODEL))],
        out_specs=main,
        out_shape=jax.ShapeDtypeStruct(x.shape, F32),
        scratch_shapes=[pltpu.VMEM((tm + 2 * halo, D_MODEL), BF16),
                        pltpu.VMEM((2, tm + 2 * halo + 2 * SUBLANE, FT), F32),
                        pltpu.VMEM((tm, FFN_DIM), BF16)],
        compiler_params=_cp(("arbitrary", "arbitrary")),
    )(x, x, x, mod, ng.reshape(1, D_MODEL), w['a'], w['v'], w['cw'], w['cb'], w['d'], fg)


def _ffn_weights(w_up, conv_w, conv_b, w_down):
    nf = FFN_DIM // FT
    tiles = lambda m: m.reshape(D_MODEL, nf, FT).transpose(1, 0, 2).astype(BF16)
    return {'a': tiles(w_up[:, :FFN_DIM]), 'v': tiles(w_up[:, FFN_DIM:]),
            'cw': conv_w.reshape(FFN_CONV * FFN_CONV, nf, FT).transpose(1, 0, 2),
            'cb': conv_b.reshape(nf, 1, FT),
            'd': w_down.astype(BF16)}


SCAN_CHUNKS = SUBLANE
SCAN_LEN = TB // SCAN_CHUNKS


def _chunk_carries(a_tot, u_tot, carry, fwd):
    row = lax.broadcasted_iota(jnp.int32, a_tot.shape, 0)
    a, u = a_tot, u_tot
    s = 1
    while s < SCAN_CHUNKS:
        has_prev = (row >= s) if fwd else (row < SCAN_CHUNKS - s)
        shift = s if fwd else SCAN_CHUNKS - s
        a_s = jnp.where(has_prev, pltpu.roll(a, shift, axis=0), 1.0)
        u_s = jnp.where(has_prev, pltpu.roll(u, shift, axis=0), 0.0)
        a, u = a * a_s, a * u_s + u
        s *= 2
    after = a * carry + u
    if fwd:
        entering = jnp.where(row >= 1, pltpu.roll(after, 1, axis=0), carry)
        return entering, after[SCAN_CHUNKS - 1:SCAN_CHUNKS, :]
    entering = jnp.where(row < SCAN_CHUNKS - 1, pltpu.roll(after, SCAN_CHUNKS - 1, axis=0), carry)
    return entering, after[0:1, :]


def _rglru_pass_kernel(fwd, with_out, *refs):
    (x_ref, xp_ref, xn_ref, mod_ref, ng_ref, wx_ref, cw_ref, cb_ref, gw_ref, gb_ref, lam_ref, perm_ref,
     h0_ref), refs = refs[:13], refs[13:]
    if fwd and with_out:
        (hb_ref, wg_ref, wo_ref, unperm_ref, o_ref), refs = refs[:5], refs[5:]
    elif not fwd:
        (o_ref,), refs = refs[:1], refs[1:]
    hfin_ref, ext_ref, a_ref, u_ref, hl_ref, p_ref, hs_ref, carry_ref = refs
    i = pl.program_id(1)
    nt = pl.num_programs(1)
    tb = i if fwd else nt - 1 - i

    @pl.when(i == 0)
    def _():
        carry_ref[...] = h0_ref[0]

    sh = mod_ref[0, 0:1, :]
    sc = mod_ref[0, 1:2, :]
    g = ng_ref[...]
    x = x_ref[0]
    h_all = jnp.concatenate([_dot(perm_ref[...], _norm_mod(x, g, sc, sh).astype(BF16)),
                             _norm_mod(xp_ref[0], g, sc, sh), _norm_mod(xn_ref[0], g, sc, sh)], axis=0).astype(BF16)
    keep_before = (tb > 0).astype(F32)
    keep_after = (tb < nt - 1).astype(F32)
    left = C_CONV // 2
    right = C_CONV - 1 - left
    row = lax.broadcasted_iota(jnp.int32, (SCAN_CHUNKS, C_BLOCK), 0)
    steps = list(range(SCAN_LEN)) if fwd else list(reversed(range(SCAN_LEN)))
    sls = [slice(hh * C_BLOCK, (hh + 1) * C_BLOCK) for hh in range(C_HEADS)]

    def project(hh):
        gate_pre = _dot(h_all[:TB], wg_ref[:, sls[hh]]) if fwd and with_out else None
        return _dot(h_all, wx_ref[:, sls[hh]]), gate_pre

    def features(hh, proj):
        sl = sls[hh]
        ext = ext_ref.at[hh % 2]
        xb = proj[:TB]
        before = proj[TB:TB + CONV_HALO] * keep_before
        after = proj[TB + CONV_HALO:] * keep_after
        ext[left * SCAN_CHUNKS:left * SCAN_CHUNKS + TB, :] = xb
        for s in range(1, left + 1):
            grp = xb[TB - s * SCAN_CHUNKS:TB - (s - 1) * SCAN_CHUNKS]
            halo = before[CONV_HALO - s:CONV_HALO - s + 1, :]
            ext[(left - s) * SCAN_CHUNKS:(left - s + 1) * SCAN_CHUNKS, :] = jnp.where(
                row == 0, halo, pltpu.roll(grp, 1, axis=0))
        for s in range(1, right + 1):
            grp = xb[(s - 1) * SCAN_CHUNKS:s * SCAN_CHUNKS]
            halo = after[s - 1:s, :]
            lo = (left + s - 1) * SCAN_CHUNKS + TB
            ext[lo:lo + SCAN_CHUNKS, :] = jnp.where(
                row == SCAN_CHUNKS - 1, halo, pltpu.roll(grp, SCAN_CHUNKS - 1, axis=0))
        xh = cb_ref[:, sl]
        for s in range(C_CONV):
            xh = xh + cw_ref[s:s + 1, sl] * ext[s * SCAN_CHUNKS:s * SCAN_CHUNKS + TB, :]
        gates = _sigmoid(_dot(xh.astype(BF16), gw_ref[hh]) + gb_ref[hh])
        a = jnp.exp(gates[:, :C_BLOCK] * ((-C_GATE_C) * _softplus(-lam_ref[:, sl])))
        a_ref[:, sl] = a
        u_ref[:, sl] = jnp.sqrt(1.0 - a * a) * (gates[:, C_BLOCK:] * xh)

    def scan(hh):
        sl = sls[hh]
        h_loc = p_loc = None
        for j in steps:
            rows = slice(j * SCAN_CHUNKS, (j + 1) * SCAN_CHUNKS)
            a_j = a_ref[rows, sl]
            u_j = u_ref[rows, sl]
            h_loc, p_loc = (u_j, a_j) if h_loc is None else (a_j * h_loc + u_j, a_j * p_loc)
            hl_ref[rows, sl] = h_loc
            p_ref[rows, sl] = p_loc
        entering, leaving = _chunk_carries(p_loc, h_loc, carry_ref[:, sl], fwd)
        carry_ref[:, sl] = leaving
        if fwd and not with_out:
            return
        for j in steps:
            rows = slice(j * SCAN_CHUNKS, (j + 1) * SCAN_CHUNKS)
            h_j = hl_ref[rows, sl] + p_ref[rows, sl] * entering
            if fwd:
                hs_ref[rows, sl] = h_j + hb_ref[0, rows, sl]
            else:
                o_ref[0, rows, sl] = h_j

    def gated_out(hh, gate_pre):
        sl = sls[hh]
        z = _dot(unperm_ref[...], (_gelu_tanh(gate_pre) * hs_ref[:, sl]).astype(BF16)).astype(BF16)
        return _dot(z, wo_ref[sl, :])

    projs = [project(0)]
    y = None
    for hh in range(C_HEADS):
        if hh + 1 < C_HEADS:
            projs.append(project(hh + 1))
        features(hh, projs[hh][0])
        if fwd and with_out and hh > 0:
            part = gated_out(hh - 1, projs[hh - 1][1])
            y = part if y is None else y + part
        scan(hh)
    if fwd and with_out:
        y = y + gated_out(C_HEADS - 1, projs[C_HEADS - 1][1])

    @pl.when(i == nt - 1)
    def _():
        hfin_ref[0] = carry_ref[...]

    if fwd and with_out:
        o_ref[0] = x + mod_ref[0, 2:3, :] * y


@functools.lru_cache(maxsize=None)
def _scan_order():
    t = np.arange(TB)
    perm = np.zeros((TB, TB), np.float32)
    perm[(t % SCAN_LEN) * SCAN_CHUNKS + t // SCAN_LEN, t] = 1.0
    return jnp.asarray(perm, BF16), jnp.asarray(perm.T, BF16)


def _rglru_pass(fwd, x, mod, per_batch, ng, wx, conv_w, conv_b, gate_w, gate_b, lam, h0, out_side=None):
    bsz, t_len, _ = x.shape
    nt = t_len // TB
    per = TB // CONV_HALO
    last = t_len // CONV_HALO - 1
    tbi = (lambda i: i) if fwd else (lambda i: nt - 1 - i)
    main = pl.BlockSpec((1, TB, D_MODEL), lambda b, i: (b, tbi(i), 0))
    prev = pl.BlockSpec((1, CONV_HALO, D_MODEL), lambda b, i: (b, jnp.maximum(tbi(i) * per - 1, 0), 0))
    nxt = pl.BlockSpec((1, CONV_HALO, D_MODEL), lambda b, i: (b, jnp.minimum((tbi(i) + 1) * per, last), 0))
    st_spec = pl.BlockSpec((1, 1, C_WIDTH), lambda b, i: (b, 0, 0))
    perm, unperm = _scan_order()
    args = [x, x, x, mod, ng.reshape(1, D_MODEL), wx, conv_w, conv_b.reshape(1, C_WIDTH), gate_w, gate_b, lam, perm,
            h0]
    in_specs = [main, prev, nxt, _mod_spec(per_batch), _const_spec((1, D_MODEL)), _const_spec(wx.shape),
                _const_spec(conv_w.shape), _const_spec((1, C_WIDTH)), _const_spec(gate_w.shape),
                _const_spec(gate_b.shape), _const_spec(lam.shape), _const_spec(perm.shape), st_spec]
    st_shape = jax.ShapeDtypeStruct((bsz, 1, C_WIDTH), F32)
    tok_shape = jax.ShapeDtypeStruct((bsz, t_len, C_WIDTH), F32)
    with_out = out_side is not None
    if fwd and with_out:
        h_bwd, wg, wo = out_side
        args += [h_bwd, wg, wo, unperm]
        in_specs += [main, _const_spec(wg.shape), _const_spec(wo.shape), _const_spec(unperm.shape)]
    if fwd and not with_out:
        out_specs, out_shape = (st_spec,), (st_shape,)
    else:
        out_specs, out_shape = (main, st_spec), (tok_shape, st_shape)
    blk = lambda: pltpu.VMEM((TB, C_WIDTH), F32)
    return pl.pallas_call(
        functools.partial(_rglru_pass_kernel, fwd, with_out),
        grid=(bsz, nt),
        in_specs=in_specs,
        out_specs=out_specs,
        out_shape=out_shape,
        scratch_shapes=[pltpu.VMEM((2, TB + (C_CONV - 1) * SCAN_CHUNKS, C_BLOCK), F32), blk(), blk(), blk(), blk(),
                        blk(),
                        pltpu.VMEM((1, C_WIDTH), F32)],
        compiler_params=_cp(("arbitrary", "arbitrary")),
    )(*args)


def _mod_rows(mods_l, bsz):
    m = mods_l.reshape(mods_l.shape[0], 6, D_MODEL)
    m = jnp.pad(m, ((0, 0), (0, SUBLANE - 6), (0, 0)))
    return m[:bsz], m[bsz:bsz + 1]


def kernel(x, c, ctx, c_ctx, mod_w, mod_b, norm_mix_g, norm_ffn_g, final_norm_g, ab_w_in, hgrn_lb, hgrn_norm_g, gdn_conv_w, gdn_a_log, gdn_dt_bias, gdn_norm_g, ab_w_out, c_w_in, c_conv_w, c_conv_b, c_gate_w, c_gate_b, c_lambda, c_w_out, ffn_w_up, ffn_conv_w, ffn_conv_b, ffn_w_down):
    bsz = x.shape[0]
    rows = 2 * SUBLANE
    cc = jnp.concatenate([c, c_ctx[None, :], jnp.zeros((rows - bsz - 1, D_MODEL), F32)], axis=0)
    mods = _modulation(cc, mod_w, mod_b)
    xc = ctx
    for l in range(DEPTH):
        last = l == DEPTH - 1
        mod_x, mod_c = _mod_rows(mods[l], bsz)
        ffn_w = _ffn_weights(ffn_w_up[l], ffn_conv_w[l], ffn_conv_b[l], ffn_w_down[l])
        if l % 2 == 0:
            e = l // 2
            w_in = ab_w_in[e]
            o = 5 * HD
            w = {'a': w_in[:, :o].astype(BF16), 'qkv': w_in[:, o:o + QKV].astype(BF16),
                 'z': w_in[:, o + QKV:o + QKV + HD].astype(BF16),
                 'alpha': w_in[:, o + QKV + HD:o + QKV + HD + 2 * B_HEADS],
                 'beta': w_in[:, o + QKV + HD + 2 * B_HEADS:]}
            w_out = ab_w_out[e].astype(BF16)
            feats = []
            for seq, mod, per_batch in ((xc, mod_c, False), (x, mod_x, True)):
                feats.append(_ab_in(l, seq, mod, per_batch, norm_mix_g[l], w, hgrn_lb, gdn_conv_w[e],
                                    gdn_a_log[e], gdn_dt_bias[e]))
            outs = []
            sa = jnp.zeros((bsz, 2, A_HEADS, A_DV, A_DK), F32)
            sb = jnp.zeros((bsz, 2, B_HEADS, B_DV, B_DK), F32)
            for qa, ka, lfa, va, ga, qb, kb, vb, zb, gbc, gbr in feats:
                oa, sa = _hgrn_scan(qa, ka, lfa, va, sa)
                ob, sb = _gdn_scan(qb, kb, vb, gbc, gbr, sb)
                outs.append((oa, ob, ga, zb))
            oa, ob, ga, zb = outs[1]
            x = _ab_out(x, oa, ob, ga, zb, mod_x, True, hgrn_norm_g[e], gdn_norm_g[e], w_out)
            if not last:
                oa, ob, ga, zb = outs[0]
                xc = _ab_out(xc, oa, ob, ga, zb, mod_c, False, hgrn_norm_g[e], gdn_norm_g[e], w_out)
        else:
            o = l // 2
            wg = c_w_in[o][:, :C_WIDTH].astype(BF16)
            wx = c_w_in[o][:, C_WIDTH:].astype(BF16)
            wo = c_w_out[o].astype(BF16)
            gate_w = c_gate_w[o].astype(BF16)
            gate_b = c_gate_b[o].reshape(2, C_HEADS, 1, 2 * C_BLOCK)
            lam = c_lambda[o].reshape(2, 1, C_WIDTH)
            zero = jnp.zeros((bsz, 1, C_WIDTH), F32)

            def run(fwd, seq, mod, per_batch, h0, out_side=None):
                d = 0 if fwd else 1
                return _rglru_pass(fwd, seq, mod, per_batch, norm_mix_g[l], wx, c_conv_w[o], c_conv_b[o],
                                   gate_w[d], gate_b[d], lam[d], h0, out_side)

            hc_bwd, sb = run(False, xc, mod_c, False, zero)
            if last:
                (sf,) = run(True, xc, mod_c, False, zero)
            else:
                xc, sf = run(True, xc, mod_c, False, zero, (hc_bwd, wg, wo))
            h_bwd, _ = run(False, x, mod_x, True, sb)
            x, _ = run(True, x, mod_x, True, sf, (h_bwd, wg, wo))
        if not last:
            xc = _ffn(xc, mod_c, False, norm_ffn_g[l], ffn_w, latent=False)
        x = _ffn(x, mod_x, True, norm_ffn_g[l], ffn_w, latent=True, final_g=final_norm_g if last else None)
    return x
```

```python
import functools
import math

import numpy as np
import jax
import jax.numpy as jnp
from jax import lax
from jax.experimental import pallas as pl
from jax.experimental.pallas import tpu as pltpu

F32 = jnp.float32
BF16 = jnp.bfloat16

D_MODEL = 1024
DEPTH = 2
GRID_W = 64
EPS = 1e-6
A_HEADS = 4
A_DK = 128
A_DV = 128
B_HEADS = 4
B_DK = 128
B_DV = 128
B_CONV = 4
C_WIDTH = D_MODEL
C_HEADS = 4
C_BLOCK = C_WIDTH // C_HEADS
C_CONV = 4
C_GATE_C = 8.0
FFN_DIM = 2816
FFN_CONV = 3

HD = A_HEADS * A_DK
QKV = 2 * B_HEADS * B_DK + B_HEADS * B_DV

LANE = 128
SUBLANE = 8
VMEM_LIMIT = 56 * 1024 * 1024

TB = 256
AB_TM = 256
SUB = 16
LEVELS = (16, 32, 64, 128)
FT = 256
FFN_TM = 1024
CONV_HALO = 8


def _cp(sem):
    return pltpu.CompilerParams(dimension_semantics=sem, vmem_limit_bytes=VMEM_LIMIT)


def _dot(a, b):
    return jnp.dot(a, b, preferred_element_type=F32)


def _dot_nt(a, b):
    return lax.dot_general(a, b, (((1,), (1,)), ((), ())), preferred_element_type=F32)


def _dot_tn(a, b):
    return lax.dot_general(a, b, (((0,), (0,)), ((), ())), preferred_element_type=F32)


def _sigmoid(x):
    return 0.5 * jnp.tanh(0.5 * x) + 0.5


def _silu(x):
    return x * _sigmoid(x)


def _softplus(x):
    return jnp.maximum(x, 0.0) + jnp.log1p(jnp.exp(-jnp.abs(x)))


def _gelu_tanh(x):
    c = math.sqrt(2.0 / math.pi)
    return 0.5 * x * (1.0 + jnp.tanh(c * (x + 0.044715 * (x * x * x))))


def _hi_lo(x):
    hi = x.astype(BF16)
    lo = (x - hi.astype(F32)).astype(BF16)
    return hi, lo


def _norm_mod(x, g, sc, sh):
    ms = jnp.mean(x * x, axis=-1, keepdims=True)
    return (x * lax.rsqrt(ms + EPS)) * g * (1.0 + sc) + sh


def _const_spec(shape):
    nd = len(shape)
    return pl.BlockSpec(shape, lambda *_: (0,) * nd, pipeline_mode=pl.Buffered(1))


MOD_TN = 1024


def _mod_kernel(c_ref, w_ref, b_ref, o_ref):
    s = _silu(c_ref[...])
    w = w_ref[0]
    s_hi, s_lo = _hi_lo(s)
    w_hi, w_lo = _hi_lo(w)
    acc = _dot(s_hi, w_hi) + _dot(s_hi, w_lo) + _dot(s_lo, w_hi)
    o_ref[0] = acc + b_ref[0]


def _modulation(cc, mod_w, mod_b):
    rows = cc.shape[0]
    n = mod_w.shape[-1]
    return pl.pallas_call(
        _mod_kernel,
        grid=(DEPTH, n // MOD_TN),
        in_specs=[pl.BlockSpec((rows, D_MODEL), lambda l, j: (0, 0)),
                  pl.BlockSpec((1, D_MODEL, MOD_TN), lambda l, j: (l, 0, j)),
                  pl.BlockSpec((1, 1, MOD_TN), lambda l, j: (l, 0, j))],
        out_specs=pl.BlockSpec((1, rows, MOD_TN), lambda l, j: (l, 0, j)),
        out_shape=jax.ShapeDtypeStruct((DEPTH, rows, n), F32),
        compiler_params=_cp(("arbitrary", "arbitrary")),
    )(cc, mod_w, mod_b.reshape(DEPTH, 1, n))


def _halo_specs(t_len, width, tm):
    per = tm // CONV_HALO
    last = t_len // CONV_HALO - 1
    main = pl.BlockSpec((1, tm, width), lambda b, i: (b, i, 0))
    prev = pl.BlockSpec((1, CONV_HALO, width), lambda b, i: (b, jnp.maximum(i * per - 1, 0), 0))
    nxt = pl.BlockSpec((1, CONV_HALO, width), lambda b, i: (b, jnp.minimum((i + 1) * per, last), 0))
    return main, prev, nxt


def _mod_spec(per_batch):
    if per_batch:
        return pl.BlockSpec((1, SUBLANE, D_MODEL), lambda b, *_: (b, 0, 0))
    return pl.BlockSpec((1, SUBLANE, D_MODEL), lambda b, *_: (0, 0, 0))


def _conv_tokens(ext_ref, w_ref, width, tm):
    left = width // 2
    acc = None
    for j in range(width):
        term = w_ref[j:j + 1, :] * ext_ref[pl.ds(CONV_HALO + j - left, tm), :]
        acc = term if acc is None else acc + term
    return acc


def _ab_in_kernel(layer, tm, x_ref, xp_ref, xn_ref, mod_ref, ng_ref, wa_ref, wqkv_ref, wz_ref, wab_ref, wabt_ref,
                  lbp_ref, convw_ref, coef_ref, coeft_ref,
                  qa_ref, ka_ref, lfa_ref, va_ref, ga_ref, qb_ref, kb_ref, vb_ref, zb_ref, gbc_ref, gbr_ref,
                  ext_ref):
    i = pl.program_id(1)
    nt = pl.num_programs(1)
    sh = mod_ref[0, 0:1, :]
    sc = mod_ref[0, 1:2, :]
    g = ng_ref[...]
    h_all = jnp.concatenate([_norm_mod(x_ref[0], g, sc, sh), _norm_mod(xp_ref[0], g, sc, sh),
                             _norm_mod(xn_ref[0], g, sc, sh)], axis=0).astype(BF16)
    hb = h_all[:tm]
    keep_p = (i > 0).astype(F32)
    keep_n = (i < nt - 1).astype(F32)

    def lower_bound(d):
        rws = [lbp_ref[d * (DEPTH + 1) + r:d * (DEPTH + 1) + r + 1, :] for r in range(DEPTH + 1)]
        mx = functools.reduce(jnp.maximum, rws)
        es = [jnp.exp(r - mx) for r in rws]
        return functools.reduce(jnp.add, es[:layer + 1]) / functools.reduce(jnp.add, es)

    def a_proj(group):
        return lambda: _dot(hb, wa_ref[:, group * HD:(group + 1) * HD])

    def put_q(p):
        qa_ref[0] = _silu(p) * (A_DK ** -0.5)

    def put_forget(d):
        def put(p):
            lb_d = lower_bound(d)
            ka_ref[d, 0] = (1.0 - lb_d) * _sigmoid(-p)
            lfa_ref[d, 0] = jnp.log(lb_d + (1.0 - lb_d) * _sigmoid(p))
        return put

    def put_v(p):
        va_ref[0] = p

    def put_gate(p):
        ga_ref[0] = _silu(p).astype(ga_ref.dtype)

    def conv_proj(part):
        def run():
            proj = _dot(h_all, wqkv_ref[:, part * HD:(part + 1) * HD])
            slot = part % 2
            ext_ref[slot, 0:CONV_HALO, :] = proj[tm:tm + CONV_HALO] * keep_p
            ext_ref[slot, CONV_HALO:CONV_HALO + tm, :] = proj[:tm]
            ext_ref[slot, CONV_HALO + tm:, :] = proj[tm + CONV_HALO:] * keep_n
        return run

    def put_conv(part):
        def put(_):
            lo = part * HD
            t_all = _silu(_conv_tokens(ext_ref.at[part % 2], convw_ref.at[:, lo:lo + HD], B_CONV, tm))
            if part == 2:
                vb_ref[0] = t_all
                return
            ref, scale = ((qb_ref, B_DK ** -0.5), (kb_ref, 1.0))[part]
            for hh in range(B_HEADS):
                t = t_all[:, hh * B_DK:(hh + 1) * B_DK]
                ss = jnp.sum(t * t, axis=-1, keepdims=True)
                ref[0, :, hh * B_DK:(hh + 1) * B_DK] = t * (lax.rsqrt(ss + EPS) * scale)
        return put

    def put_z(p):
        zb_ref[0] = _silu(p).astype(zb_ref.dtype)

    def gate_proj():
        return _dot(hb, wab_ref[...]), _dot_nt(wabt_ref[...], hb)

    def put_gates(p):
        ab, abt = p
        lane = lax.broadcasted_iota(jnp.int32, ab.shape, 1)
        is_g = (lane % SUBLANE) < B_HEADS
        act = jnp.where(is_g, -jnp.exp(coef_ref[0:1, :]) * _softplus(ab + coef_ref[1:2, :]), _sigmoid(ab))
        row = lax.broadcasted_iota(jnp.int32, abt.shape, 0)
        is_g_t = (row % SUBLANE) < B_HEADS
        act_t = jnp.where(is_g_t, -jnp.exp(coeft_ref[:, 0:1]) * _softplus(abt + coeft_ref[:, 1:2]), _sigmoid(abt))
        col = lax.broadcasted_iota(jnp.int32, (tm, LANE), 1)
        for d in range(2):
            shifted = act if d == 0 else pltpu.roll(act, LANE - SUBLANE, axis=1)
            gbc_ref[0, d] = jnp.where(col < SUBLANE, shifted, 0.0)
            gbr_ref[0, d] = act_t[d * SUBLANE:(d + 1) * SUBLANE, :]

    stages = [(a_proj(0), put_q), (a_proj(1), put_forget(0)), (a_proj(2), put_forget(1)), (a_proj(3), put_v),
              (a_proj(4), put_gate), (conv_proj(0), put_conv(0)), (conv_proj(1), put_conv(1)),
              (conv_proj(2), put_conv(2)), (lambda: _dot(hb, wz_ref[...]), put_z), (gate_proj, put_gates)]
    pending = stages[0][0]()
    for s, (_, consume) in enumerate(stages):
        nxt = stages[s + 1][0]() if s + 1 < len(stages) else None
        consume(pending)
        pending = nxt


def _ab_in(layer, x, mod, per_batch, ng, w, hgrn_lb, conv_w, a_log, dt_bias):
    bsz, t_len, _ = x.shape
    tm = min(AB_TM, t_len)
    nt = t_len // tm
    main, prev, nxt = _halo_specs(t_len, D_MODEL, tm)
    tok = lambda width: pl.BlockSpec((1, tm, width), lambda b, i: (b, i, 0))
    tok2 = lambda width: pl.BlockSpec((2, 1, tm, width), lambda b, i: (0, b, i, 0))
    sds = lambda *s: jax.ShapeDtypeStruct(s, F32)
    gate_sds = jax.ShapeDtypeStruct((bsz, t_len, HD), BF16)
    out_shape = (sds(bsz, t_len, HD), sds(2, bsz, t_len, HD), sds(2, bsz, t_len, HD), sds(bsz, t_len, HD),
                 gate_sds, sds(bsz, t_len, HD), sds(bsz, t_len, HD), sds(bsz, t_len, HD),
                 gate_sds, sds(bsz, 2, t_len, LANE), sds(bsz, 2, SUBLANE, t_len))
    out_specs = (tok(HD), tok2(HD), tok2(HD), tok(HD), tok(HD), tok(HD), tok(HD), tok(HD), tok(HD),
                 pl.BlockSpec((1, 2, tm, LANE), lambda b, i: (b, 0, i, 0)),
                 pl.BlockSpec((1, 2, SUBLANE, tm), lambda b, i: (b, 0, 0, i)))
    w_al, w_be = w['alpha'], w['beta']
    cols = [w_al[:, 0:4], w_be[:, 0:4], w_al[:, 4:8], w_be[:, 4:8]]
    wab = jnp.concatenate(cols, axis=1)
    wab_p = jnp.pad(wab, ((0, 0), (0, LANE - 2 * SUBLANE))).astype(BF16)
    wabt = wab.T.astype(BF16)
    zero4 = jnp.zeros((B_HEADS,), F32)
    a_flat = jnp.concatenate([a_log[0], zero4, a_log[1], zero4])
    d_flat = jnp.concatenate([dt_bias[0], zero4, dt_bias[1], zero4])
    coef = jnp.pad(jnp.stack([a_flat, d_flat]), ((0, 0), (0, LANE - 2 * SUBLANE)))
    coeft = jnp.stack([a_flat, d_flat], axis=1)
    hgrn_lb = hgrn_lb.reshape(2 * (DEPTH + 1), HD)
    args = (x, x, x, mod, ng.reshape(1, D_MODEL), w['a'], w['qkv'], w['z'], wab_p, wabt, hgrn_lb, conv_w, coef,
            coeft)
    in_specs = [main, prev, nxt, _mod_spec(per_batch), _const_spec((1, D_MODEL)), _const_spec(w['a'].shape),
                _const_spec(w['qkv'].shape), _const_spec(w['z'].shape), _const_spec(wab_p.shape),
                _const_spec(wabt.shape), _const_spec(hgrn_lb.shape), _const_spec(conv_w.shape),
                _const_spec(coef.shape), _const_spec(coeft.shape)]
    return pl.pallas_call(
        functools.partial(_ab_in_kernel, layer, tm),
        grid=(bsz, nt),
        in_specs=in_specs,
        out_specs=out_specs,
        out_shape=out_shape,
        scratch_shapes=[pltpu.VMEM((2, tm + 2 * CONV_HALO, HD), F32)],
        compiler_params=_cp(("arbitrary", "arbitrary")),
    )(*args)


def _both_dirs(stack):
    return np.stack([stack, stack[:, ::-1, ::-1]])


@functools.lru_cache(maxsize=None)
def _hgrn_consts():
    t = np.arange(TB)
    i, j = t[:, None], t[None, :]
    masks = [(i // SUB == j // SUB) & (j <= i)]
    for m in LEVELS:
        seg = 2 * m
        masks.append(((i // seg) == (j // seg)) & ((i % seg) >= m) & ((j % seg) < m))
    masks = _both_dirs(np.stack(masks).astype(np.float32))
    tri = _both_dirs((j <= i).astype(np.float32)[None])[:, 0]
    return jnp.asarray(tri, BF16), jnp.asarray(masks, F32)


@functools.lru_cache(maxsize=None)
def _gdn_consts():
    t = np.arange(TB)
    i, j = t[:, None], t[None, :]
    tri = _both_dirs(np.stack([j <= i, j < i]).astype(np.float32))
    blocks = [(i // SUB) == (j // SUB)]
    for m in LEVELS:
        seg = 2 * m
        blocks.append(((i // seg) == (j // seg)) & ((i // m) != (j // m)))
    blocks = np.stack(blocks).astype(np.float32)
    return jnp.asarray(tri[:, 0], BF16), jnp.asarray(tri, F32), jnp.asarray(blocks, F32)


def _time_block(d, i, nt):
    return i + d * (nt - 1 - 2 * i)


def _segment_exponents(xp, fwd):
    def row(r):
        return xp[r:r + 1, :]

    pieces = []
    for k in range(TB // SUB):
        rows = xp[k * SUB:(k + 1) * SUB]
        r = k * SUB - 1 if fwd else (k + 1) * SUB
        pieces.append(rows - row(r) if 0 <= r < TB else rows)
    out = [jnp.concatenate(pieces, axis=0)]
    for m in LEVELS:
        pieces = []
        for k in range(TB // (2 * m)):
            lo = k * 2 * m
            first, second = xp[lo:lo + m], xp[lo + m:lo + 2 * m]
            if fwd:
                c = row(lo + m - 1)
                pieces += [c - first, second - c]
            else:
                c = row(lo + m)
                pieces += [first - c, c - second]
        out.append(jnp.concatenate(pieces, axis=0))
    return out


def _hgrn_scan_kernel(q_ref, k_ref, lf_ref, v_ref, tri_ref, masks_ref, s0_ref, o_ref, sfin_ref, st_ref):
    d = pl.program_id(1)
    i = pl.program_id(2)
    nt = pl.num_programs(2)

    @pl.when(i == 0)
    def _():
        st_ref[...] = s0_ref[0, 0]

    n_lv = len(LEVELS)
    heads = range(A_HEADS)
    sls = [slice(hh * A_DK, (hh + 1) * A_DK) for hh in heads]

    def block(fwd):
        xps, tots = [], []
        for hh in heads:
            lf = lf_ref[0, 0, :, sls[hh]]
            lf_hi, lf_lo = _hi_lo(lf)
            xx = _dot(tri_ref[0], jnp.concatenate([lf_hi, lf_lo], axis=1))
            xps.append(xx[:, :A_DK] + xx[:, A_DK:])
            tots.append(jnp.sum(lf, axis=0, keepdims=True))
        exps = [_segment_exponents(xps[hh], fwd) for hh in heads]
        attns = []
        for hh in heads:
            x0 = exps[hh][0]
            raw = _dot_nt((q_ref[0, :, sls[hh]] * jnp.exp(x0)).astype(BF16),
                          (k_ref[0, 0, :, sls[hh]] * jnp.exp(-x0)).astype(BF16))
            attns.append(jnp.where(masks_ref[0, 0] > 0.0, raw, 0.0))
        for lv in range(1, n_lv + 1):
            for hh in heads:
                e = jnp.exp(exps[hh][lv])
                attns[hh] = attns[hh] + masks_ref[0, lv] * _dot_nt((q_ref[0, :, sls[hh]] * e).astype(BF16),
                                                                   (k_ref[0, 0, :, sls[hh]] * e).astype(BF16))
        sts = [st_ref[hh] for hh in heads]
        for hh in heads:
            e_in = jnp.exp(xps[hh])
            o = (_dot(attns[hh].astype(BF16), v_ref[0, :, sls[hh]].astype(BF16))
                 + _dot_nt((q_ref[0, :, sls[hh]] * e_in).astype(BF16), sts[hh].astype(BF16)))
            o_ref[0, 0, :, sls[hh]] = o.astype(o_ref.dtype)
        for hh in heads:
            e_out = jnp.exp(tots[hh] - xps[hh])
            st_ref[hh] = sts[hh] * jnp.exp(tots[hh]) + _dot_tn(v_ref[0, :, sls[hh]].astype(BF16),
                                                               (k_ref[0, 0, :, sls[hh]] * e_out).astype(BF16))

    @pl.when(d == 0)
    def _():
        block(True)

    @pl.when(d == 1)
    def _():
        block(False)

    @pl.when(i == nt - 1)
    def _():
        sfin_ref[0, 0] = st_ref[...]


def _hgrn_scan(q, k2, lf2, v, s0):
    bsz, t_len, _ = q.shape
    nt = t_len // TB
    tri, masks = _hgrn_consts()
    tok = pl.BlockSpec((1, TB, HD), lambda b, d, i: (b, _time_block(d, i, nt), 0))
    tok2 = pl.BlockSpec((1, 1, TB, HD), lambda b, d, i: (d, b, _time_block(d, i, nt), 0))
    st_spec = pl.BlockSpec((1, 1, A_HEADS, A_DV, A_DK), lambda b, d, i: (b, d, 0, 0, 0))
    return pl.pallas_call(
        _hgrn_scan_kernel,
        grid=(bsz, 2, nt),
        in_specs=[tok, tok2, tok2, tok,
                  pl.BlockSpec((1, TB, TB), lambda b, d, i: (d, 0, 0)),
                  pl.BlockSpec((1,) + masks.shape[1:], lambda b, d, i: (d, 0, 0, 0)),
                  st_spec],
        out_specs=(tok2, st_spec),
        out_shape=(jax.ShapeDtypeStruct((2, bsz, t_len, HD), BF16),
                   jax.ShapeDtypeStruct((bsz, 2, A_HEADS, A_DV, A_DK), F32)),
        scratch_shapes=[pltpu.VMEM((A_HEADS, A_DV, A_DK), F32)],
        compiler_params=_cp(("arbitrary", "arbitrary", "arbitrary")),
    )(q, k2, lf2, v, tri, masks, s0)


def _unit_tri_inverse_minus_identity(a_list, blocks_ref):
    b16 = lambda x: x.astype(BF16)
    ps = [a * blocks_ref[0] for a in a_list]
    es = [-p for p in ps]
    width = 1
    while 2 * width < SUB:
        ps = [_dot(b16(p), b16(p)) for p in ps]
        es = [e + p + _dot(b16(e), b16(p)) for e, p in zip(es, ps)]
        width *= 2
    for lv in range(len(LEVELS)):
        bms = [a * blocks_ref[lv + 1] for a in a_list]
        ys = [bm + _dot(b16(e), b16(bm)) for e, bm in zip(es, bms)]
        es = [e - (y + _dot(b16(y), b16(e))) for e, y in zip(es, ys)]
    return es


def _gdn_scan_kernel(q_ref, k_ref, v_ref, gbc_ref, gbr_ref, linc_ref, tri_ref, blocks_ref, s0_ref,
                     o_ref, sfin_ref, st_ref):
    i = pl.program_id(2)
    nt = pl.num_programs(2)

    @pl.when(i == 0)
    def _():
        st_ref[...] = s0_ref[0, 0]

    gcol = gbc_ref[0, 0]
    grow = gbr_ref[0, 0]
    linc = linc_ref[0]
    c_hi, c_lo = _hi_lo(gcol)
    gc_col = _dot(linc, jnp.concatenate([c_hi, c_lo], axis=1))
    gc_col = gc_col[:, :LANE] + gc_col[:, LANE:]
    r_hi, r_lo = _hi_lo(grow)
    gc_row = _dot_nt(jnp.concatenate([r_hi, r_lo], axis=0), linc)
    gc_row = gc_row[:SUBLANE] + gc_row[SUBLANE:]
    g_tot = jnp.sum(gcol, axis=0, keepdims=True)
    incl = tri_ref[0, 0] > 0.0
    strict = tri_ref[0, 1]

    heads = range(B_HEADS)
    sls = [slice(hh * B_DK, (hh + 1) * B_DK) for hh in heads]
    gccs = [gc_col[:, hh:hh + 1] for hh in heads]
    tots = [g_tot[:, hh:hh + 1] for hh in heads]
    betas = [gcol[:, B_HEADS + hh:B_HEADS + hh + 1] for hh in heads]
    a_list, attns, kbetas = [], [], []
    for hh in heads:
        q = q_ref[0, :, sls[hh]]
        k = k_ref[0, :, sls[hh]]
        gcr = gc_row[hh:hh + 1, :]
        decay = jnp.where(incl, jnp.exp(jnp.where(incl, gccs[hh] - gcr, 0.0)), 0.0)
        kbeta = k * betas[hh]
        prod = _dot_nt(jnp.concatenate([kbeta, q], axis=0).astype(BF16), k.astype(BF16))
        a_list.append(strict * prod[:TB] * decay)
        attns.append(prod[TB:] * decay)
        kbetas.append(kbeta)
    es = _unit_tri_inverse_minus_identity(a_list, blocks_ref)
    egs = [jnp.exp(gccs[hh]) for hh in heads]
    sols = []
    for hh in heads:
        rhs = jnp.concatenate([v_ref[0, :, sls[hh]] * betas[hh], kbetas[hh] * egs[hh]], axis=1)
        sols.append(rhs + _dot(es[hh].astype(BF16), rhs.astype(BF16)))
    sts = [st_ref[hh] for hh in heads]
    ws_qs = [_dot_nt(jnp.concatenate([sols[hh][:, B_DV:], q_ref[0, :, sls[hh]] * egs[hh]], axis=0).astype(BF16),
                     sts[hh].astype(BF16)) for hh in heads]
    v_news = [sols[hh][:, :B_DV] - ws_qs[hh][:TB] for hh in heads]
    for hh in heads:
        o = ws_qs[hh][TB:] + _dot(attns[hh].astype(BF16), v_news[hh].astype(BF16))
        o_ref[0, 0, :, sls[hh]] = o.astype(o_ref.dtype)
    for hh in heads:
        kd = k_ref[0, :, sls[hh]] * jnp.exp(tots[hh] - gccs[hh])
        st_ref[hh] = sts[hh] * jnp.exp(tots[hh]) + _dot_tn(v_news[hh].astype(BF16), kd.astype(BF16))

    @pl.when(i == nt - 1)
    def _():
        sfin_ref[0, 0] = st_ref[...]


def _gdn_scan(q, k, v, gbc, gbr, s0):
    bsz, t_len, _ = q.shape
    nt = t_len // TB
    linc, tri, blocks = _gdn_consts()
    tok = pl.BlockSpec((1, TB, HD), lambda b, d, i: (b, _time_block(d, i, nt), 0))
    tok2 = pl.BlockSpec((1, 1, TB, HD), lambda b, d, i: (d, b, _time_block(d, i, nt), 0))
    st_spec = pl.BlockSpec((1, 1, B_HEADS, B_DV, B_DK), lambda b, d, i: (b, d, 0, 0, 0))
    return pl.pallas_call(
        _gdn_scan_kernel,
        grid=(bsz, 2, nt),
        in_specs=[tok, tok, tok,
                  pl.BlockSpec((1, 1, TB, LANE), lambda b, d, i: (b, d, _time_block(d, i, nt), 0)),
                  pl.BlockSpec((1, 1, SUBLANE, TB), lambda b, d, i: (b, d, 0, _time_block(d, i, nt))),
                  pl.BlockSpec((1, TB, TB), lambda b, d, i: (d, 0, 0)),
                  pl.BlockSpec((1, 2, TB, TB), lambda b, d, i: (d, 0, 0, 0)),
                  _const_spec(blocks.shape),
                  st_spec],
        out_specs=(tok2, st_spec),
        out_shape=(jax.ShapeDtypeStruct((2, bsz, t_len, HD), BF16),
                   jax.ShapeDtypeStruct((bsz, 2, B_HEADS, B_DV, B_DK), F32)),
        scratch_shapes=[pltpu.VMEM((B_HEADS, B_DV, B_DK), F32)],
        compiler_params=_cp(("arbitrary", "arbitrary", "arbitrary")),
    )(q, k, v, gbc, gbr, linc, tri, blocks, s0)


def _head_norm_gate(o, g, gate, heads, width):
    parts = []
    for hh in range(heads):
        t = o[:, hh * width:(hh + 1) * width]
        ms = jnp.mean(t * t, axis=-1, keepdims=True)
        parts.append(t * lax.rsqrt(ms + EPS) * g)
    return jnp.concatenate(parts, axis=1) * gate


def _ab_out_kernel(x_ref, oaf_ref, oab_ref, obf_ref, obb_ref, ga_ref, zb_ref, mod_ref, ag_ref, bg_ref, w_ref,
                   o_ref):
    f32 = lambda ref, *idx: ref[idx].astype(F32)
    ya = _head_norm_gate(f32(oaf_ref, 0, 0) + f32(oab_ref, 0, 0), ag_ref[...], f32(ga_ref, 0), A_HEADS, A_DV)
    yb = _head_norm_gate(f32(obf_ref, 0, 0) + f32(obb_ref, 0, 0), bg_ref[...], f32(zb_ref, 0), B_HEADS, B_DV)
    y = _dot(jnp.concatenate([ya, yb], axis=1).astype(BF16), w_ref[...])
    o_ref[0] = x_ref[0] + mod_ref[0, 2:3, :] * y


def _ab_out(x, oa, ob, ga, zb, mod, per_batch, hgrn_g, gdn_g, w_out):
    bsz, t_len, _ = x.shape
    tok = lambda width: pl.BlockSpec((1, TB, width), lambda b, i: (b, i, 0))
    dir_spec = lambda d: pl.BlockSpec((1, 1, TB, HD), lambda b, i: (d, b, i, 0))
    return pl.pallas_call(
        _ab_out_kernel,
        grid=(bsz, t_len // TB),
        in_specs=[tok(D_MODEL), dir_spec(0), dir_spec(1), dir_spec(0), dir_spec(1), tok(HD), tok(HD),
                  _mod_spec(per_batch), _const_spec((1, A_DV)), _const_spec((1, B_DV)),
                  _const_spec(w_out.shape)],
        out_specs=tok(D_MODEL),
        out_shape=jax.ShapeDtypeStruct(x.shape, F32),
        compiler_params=_cp(("arbitrary", "arbitrary")),
    )(x, oa, oa, ob, ob, ga, zb, mod, hgrn_g.reshape(1, A_DV), gdn_g.reshape(1, B_DV), w_out)


def _ffn_kernel(tm, halo, row_w, final, x_ref, xp_ref, xn_ref, mod_ref, ng_ref, wa_ref, wv_ref, cw_ref, cb_ref,
                wd_ref, fg_ref, o_ref, h_ref, a_ref, g_ref):
    i = pl.program_id(1)
    nt = pl.num_programs(1)
    sh = mod_ref[0, 3:4, :]
    sc = mod_ref[0, 4:5, :]
    g = ng_ref[...]
    x = x_ref[0]
    h_ref[halo:halo + tm, :] = _norm_mod(x, g, sc, sh).astype(BF16)
    if halo:
        h_ref[0:halo, :] = _norm_mod(xp_ref[0], g, sc, sh).astype(BF16)
        h_ref[halo + tm:, :] = _norm_mod(xn_ref[0], g, sc, sh).astype(BF16)
        keep_p = (i > 0).astype(F32)
        keep_n = (i < nt - 1).astype(F32)
    pad = SUBLANE
    zeros = jnp.zeros((pad, FT), F32)
    for slot in range(2):
        a_ref[slot, 0:pad, :] = zeros
        a_ref[slot, pad + tm + 2 * halo:, :] = zeros
    col = lax.broadcasted_iota(jnp.int32, (tm, FT), 0) % row_w
    has_left = col >= 1
    has_right = col <= row_w - 2
    dys = (-1, 0, 1) if halo else (0,)
    n_tiles = FFN_DIM // FT

    def up(f):
        a = _dot(h_ref[...], wa_ref[f])
        slot = f % 2
        if halo:
            a_ref[slot, pad:pad + halo, :] = a[0:halo] * keep_p
            a_ref[slot, pad + halo:pad + halo + tm, :] = a[halo:halo + tm]
            a_ref[slot, pad + halo + tm:pad + tm + 2 * halo, :] = a[halo + tm:] * keep_n
        else:
            a_ref[slot, pad:pad + tm, :] = a
        return _dot(h_ref[halo:halo + tm, :], wv_ref[f])

    val = up(0)
    for f in range(n_tiles):
        nxt_val = up(f + 1) if f + 1 < n_tiles else None
        slot = f % 2
        cw = cw_ref[f]
        cols = []
        for dx in (-1, 0, 1):
            s = None
            for dy in dys:
                tap = cw[(dy + 1) * FFN_CONV + dx + 1:(dy + 1) * FFN_CONV + dx + 2, :]
                term = tap * a_ref[slot, pl.ds(pad + halo + dy * row_w + dx, tm), :]
                s = term if s is None else s + term
            cols.append(s)
        conv = cols[1] + jnp.where(has_left, cols[0], 0.0) + jnp.where(has_right, cols[2], 0.0)
        g_ref[:, f * FT:(f + 1) * FT] = (_silu(conv + cb_ref[f]) * val).astype(BF16)
        val = nxt_val
    y = x + mod_ref[0, 5:6, :] * _dot(g_ref[...], wd_ref[...])
    if final:
        ms = jnp.mean(y * y, axis=-1, keepdims=True)
        y = y * lax.rsqrt(ms + EPS) * fg_ref[...]
    o_ref[0] = y


def _ffn(x, mod, per_batch, ng, w, latent, final_g=None):
    bsz, t_len, _ = x.shape
    tm = FFN_TM if latent else t_len
    halo = GRID_W if latent else 0
    row_w = GRID_W if latent else t_len
    hb = max(halo, SUBLANE)
    per = tm // hb
    last = t_len // hb - 1
    main = pl.BlockSpec((1, tm, D_MODEL), lambda b, i: (b, i, 0))
    prev = pl.BlockSpec((1, hb, D_MODEL), lambda b, i: (b, jnp.maximum(i * per - 1, 0), 0))
    nxt = pl.BlockSpec((1, hb, D_MODEL), lambda b, i: (b, jnp.minimum((i + 1) * per, last), 0))
    final = final_g is not None
    fg = (final_g if final else jnp.ones((D_MODEL,), F32)).reshape(1, D_MODEL)
    kern = functools.partial(_ffn_kernel, tm, halo, row_w, final)
    return pl.pallas_call(
        kern,
        grid=(bsz, t_len // tm),
        in_specs=[main, prev, nxt, _mod_spec(per_batch), _const_spec((1, D_MODEL)),
                  _const_spec(w['a'].shape), _const_spec(w['v'].shape), _const_spec(w['cw'].shape),
                  _const_spec(w['cb'].shape), _const_spec(w['d'].shape), _const_spec((1, D_MODEL))],
        out_specs=main,
        out_shape=jax.ShapeDtypeStruct(x.shape, F32),
        scratch_shapes=[pltpu.VMEM((tm + 2 * halo, D_MODEL), BF16),
                        pltpu.VMEM((2, tm + 2 * halo + 2 * SUBLANE, FT), F32),
                        pltpu.VMEM((tm, FFN_DIM), BF16)],
        compiler_params=_cp(("arbitrary", "arbitrary")),
    )(x, x, x, mod, ng.reshape(1, D_MODEL), w['a'], w['v'], w['cw'], w['cb'], w['d'], fg)


def _ffn_weights(w_up, conv_w, conv_b, w_down):
    nf = FFN_DIM // FT
    tiles = lambda m: m.reshape(D_MODEL, nf, FT).transpose(1, 0, 2).astype(BF16)
    return {'a': tiles(w_up[:, :FFN_DIM]), 'v': tiles(w_up[:, FFN_DIM:]),
            'cw': conv_w.reshape(FFN_CONV * FFN_CONV, nf, FT).transpose(1, 0, 2),
            'cb': conv_b.reshape(nf, 1, FT),
            'd': w_down.astype(BF16)}


SCAN_CHUNKS = SUBLANE
SCAN_LEN = TB // SCAN_CHUNKS


def _chunk_carries(a_tot, u_tot, carry, fwd):
    row = lax.broadcasted_iota(jnp.int32, a_tot.shape, 0)
    a, u = a_tot, u_tot
    s = 1
    while s < SCAN_CHUNKS:
        has_prev = (row >= s) if fwd else (row < SCAN_CHUNKS - s)
        shift = s if fwd else SCAN_CHUNKS - s
        a_s = jnp.where(has_prev, pltpu.roll(a, shift, axis=0), 1.0)
        u_s = jnp.where(has_prev, pltpu.roll(u, shift, axis=0), 0.0)
        a, u = a * a_s, a * u_s + u
        s *= 2
    after = a * carry + u
    if fwd:
        entering = jnp.where(row >= 1, pltpu.roll(after, 1, axis=0), carry)
        return entering, after[SCAN_CHUNKS - 1:SCAN_CHUNKS, :]
    entering = jnp.where(row < SCAN_CHUNKS - 1, pltpu.roll(after, SCAN_CHUNKS - 1, axis=0), carry)
    return entering, after[0:1, :]


def _rglru_pass_kernel(fwd, with_out, *refs):
    (x_ref, xp_ref, xn_ref, mod_ref, ng_ref, wx_ref, cw_ref, cb_ref, gw_ref, gb_ref, lam_ref, perm_ref,
     h0_ref), refs = refs[:13], refs[13:]
    if fwd and with_out:
        (hb_ref, wg_ref, wo_ref, unperm_ref, o_ref), refs = refs[:5], refs[5:]
    elif not fwd:
        (o_ref,), refs = refs[:1], refs[1:]
    hfin_ref, ext_ref, a_ref, u_ref, hl_ref, p_ref, hs_ref, carry_ref = refs
    i = pl.program_id(1)
    nt = pl.num_programs(1)
    tb = i if fwd else nt - 1 - i

    @pl.when(i == 0)
    def _():
        carry_ref[...] = h0_ref[0]

    sh = mod_ref[0, 0:1, :]
    sc = mod_ref[0, 1:2, :]
    g = ng_ref[...]
    x = x_ref[0]
    h_all = jnp.concatenate([_dot(perm_ref[...], _norm_mod(x, g, sc, sh).astype(BF16)),
                             _norm_mod(xp_ref[0], g, sc, sh), _norm_mod(xn_ref[0], g, sc, sh)], axis=0).astype(BF16)
    keep_before = (tb > 0).astype(F32)
    keep_after = (tb < nt - 1).astype(F32)
    left = C_CONV // 2
    right = C_CONV - 1 - left
    row = lax.broadcasted_iota(jnp.int32, (SCAN_CHUNKS, C_BLOCK), 0)
    steps = list(range(SCAN_LEN)) if fwd else list(reversed(range(SCAN_LEN)))
    sls = [slice(hh * C_BLOCK, (hh + 1) * C_BLOCK) for hh in range(C_HEADS)]

    def project(hh):
        gate_pre = _dot(h_all[:TB], wg_ref[:, sls[hh]]) if fwd and with_out else None
        return _dot(h_all, wx_ref[:, sls[hh]]), gate_pre

    def features(hh, proj):
        sl = sls[hh]
        ext = ext_ref.at[hh % 2]
        xb = proj[:TB]
        before = proj[TB:TB + CONV_HALO] * keep_before
        after = proj[TB + CONV_HALO:] * keep_after
        ext[left * SCAN_CHUNKS:left * SCAN_CHUNKS + TB, :] = xb
        for s in range(1, left + 1):
            grp = xb[TB - s * SCAN_CHUNKS:TB - (s - 1) * SCAN_CHUNKS]
            halo = before[CONV_HALO - s:CONV_HALO - s + 1, :]
            ext[(left - s) * SCAN_CHUNKS:(left - s + 1) * SCAN_CHUNKS, :] = jnp.where(
                row == 0, halo, pltpu.roll(grp, 1, axis=0))
        for s in range(1, right + 1):
            grp = xb[(s - 1) * SCAN_CHUNKS:s * SCAN_CHUNKS]
            halo = after[s - 1:s, :]
            lo = (left + s - 1) * SCAN_CHUNKS + TB
            ext[lo:lo + SCAN_CHUNKS, :] = jnp.where(
                row == SCAN_CHUNKS - 1, halo, pltpu.roll(grp, SCAN_CHUNKS - 1, axis=0))
        xh = cb_ref[:, sl]
        for s in range(C_CONV):
            xh = xh + cw_ref[s:s + 1, sl] * ext[s * SCAN_CHUNKS:s * SCAN_CHUNKS + TB, :]
        gates = _sigmoid(_dot(xh.astype(BF16), gw_ref[hh]) + gb_ref[hh])
        a = jnp.exp(gates[:, :C_BLOCK] * ((-C_GATE_C) * _softplus(-lam_ref[:, sl])))
        a_ref[:, sl] = a
        u_ref[:, sl] = jnp.sqrt(1.0 - a * a) * (gates[:, C_BLOCK:] * xh)

    def scan(hh):
        sl = sls[hh]
        h_loc = p_loc = None
        for j in steps:
            rows = slice(j * SCAN_CHUNKS, (j + 1) * SCAN_CHUNKS)
            a_j = a_ref[rows, sl]
            u_j = u_ref[rows, sl]
            h_loc, p_loc = (u_j, a_j) if h_loc is None else (a_j * h_loc + u_j, a_j * p_loc)
            hl_ref[rows, sl] = h_loc
            p_ref[rows, sl] = p_loc
        entering, leaving = _chunk_carries(p_loc, h_loc, carry_ref[:, sl], fwd)
        carry_ref[:, sl] = leaving
        if fwd and not with_out:
            return
        for j in steps:
            rows = slice(j * SCAN_CHUNKS, (j + 1) * SCAN_CHUNKS)
            h_j = hl_ref[rows, sl] + p_ref[rows, sl] * entering
            if fwd:
                hs_ref[rows, sl] = h_j + hb_ref[0, rows, sl]
            else:
                o_ref[0, rows, sl] = h_j

    def gated_out(hh, gate_pre):
        sl = sls[hh]
        z = _dot(unperm_ref[...], (_gelu_tanh(gate_pre) * hs_ref[:, sl]).astype(BF16)).astype(BF16)
        return _dot(z, wo_ref[sl, :])

    projs = [project(0)]
    y = None
    for hh in range(C_HEADS):
        if hh + 1 < C_HEADS:
            projs.append(project(hh + 1))
        features(hh, projs[hh][0])
        if fwd and with_out and hh > 0:
            part = gated_out(hh - 1, projs[hh - 1][1])
            y = part if y is None else y + part
        scan(hh)
    if fwd and with_out:
        y = y + gated_out(C_HEADS - 1, projs[C_HEADS - 1][1])

    @pl.when(i == nt - 1)
    def _():
        hfin_ref[0] = carry_ref[...]

    if fwd and with_out:
        o_ref[0] = x + mod_ref[0, 2:3, :] * y


@functools.lru_cache(maxsize=None)
def _scan_order():
    t = np.arange(TB)
    perm = np.zeros((TB, TB), np.float32)
    perm[(t % SCAN_LEN) * SCAN_CHUNKS + t // SCAN_LEN, t] = 1.0
    return jnp.asarray(perm, BF16), jnp.asarray(perm.T, BF16)


def _rglru_pass(fwd, x, mod, per_batch, ng, wx, conv_w, conv_b, gate_w, gate_b, lam, h0, out_side=None):
    bsz, t_len, _ = x.shape
    nt = t_len // TB
    per = TB // CONV_HALO
    last = t_len // CONV_HALO - 1
    tbi = (lambda i: i) if fwd else (lambda i: nt - 1 - i)
    main = pl.BlockSpec((1, TB, D_MODEL), lambda b, i: (b, tbi(i), 0))
    prev = pl.BlockSpec((1, CONV_HALO, D_MODEL), lambda b, i: (b, jnp.maximum(tbi(i) * per - 1, 0), 0))
    nxt = pl.BlockSpec((1, CONV_HALO, D_MODEL), lambda b, i: (b, jnp.minimum((tbi(i) + 1) * per, last), 0))
    st_spec = pl.BlockSpec((1, 1, C_WIDTH), lambda b, i: (b, 0, 0))
    perm, unperm = _scan_order()
    args = [x, x, x, mod, ng.reshape(1, D_MODEL), wx, conv_w, conv_b.reshape(1, C_WIDTH), gate_w, gate_b, lam, perm,
            h0]
    in_specs = [main, prev, nxt, _mod_spec(per_batch), _const_spec((1, D_MODEL)), _const_spec(wx.shape),
                _const_spec(conv_w.shape), _const_spec((1, C_WIDTH)), _const_spec(gate_w.shape),
                _const_spec(gate_b.shape), _const_spec(lam.shape), _const_spec(perm.shape), st_spec]
    st_shape = jax.ShapeDtypeStruct((bsz, 1, C_WIDTH), F32)
    tok_shape = jax.ShapeDtypeStruct((bsz, t_len, C_WIDTH), F32)
    with_out = out_side is not None
    if fwd and with_out:
        h_bwd, wg, wo = out_side
        args += [h_bwd, wg, wo, unperm]
        in_specs += [main, _const_spec(wg.shape), _const_spec(wo.shape), _const_spec(unperm.shape)]
    if fwd and not with_out:
        out_specs, out_shape = (st_spec,), (st_shape,)
    else:
        out_specs, out_shape = (main, st_spec), (tok_shape, st_shape)
    blk = lambda: pltpu.VMEM((TB, C_WIDTH), F32)
    return pl.pallas_call(
        functools.partial(_rglru_pass_kernel, fwd, with_out),
        grid=(bsz, nt),
        in_specs=in_specs,
        out_specs=out_specs,
        out_shape=out_shape,
        scratch_shapes=[pltpu.VMEM((2, TB + (C_CONV - 1) * SCAN_CHUNKS, C_BLOCK), F32), blk(), blk(), blk(), blk(),
                        blk(),
                        pltpu.VMEM((1, C_WIDTH), F32)],
        compiler_params=_cp(("arbitrary", "arbitrary")),
    )(*args)


def _mod_rows(mods_l, bsz):
    m = mods_l.reshape(mods_l.shape[0], 6, D_MODEL)
    m = jnp.pad(m, ((0, 0), (0, SUBLANE - 6), (0, 0)))
    return m[:bsz], m[bsz:bsz + 1]


def kernel(x, c, ctx, c_ctx, mod_w, mod_b, norm_mix_g, norm_ffn_g, final_norm_g, ab_w_in, hgrn_lb, hgrn_norm_g, gdn_conv_w, gdn_a_log, gdn_dt_bias, gdn_norm_g, ab_w_out, c_w_in, c_conv_w, c_conv_b, c_gate_w, c_gate_b, c_lambda, c_w_out, ffn_w_up, ffn_conv_w, ffn_conv_b, ffn_w_down):
    bsz = x.shape[0]
    rows = 2 * SUBLANE
    cc = jnp.concatenate([c, c_ctx[None, :], jnp.zeros((rows - bsz - 1, D_MODEL), F32)], axis=0)
    mods = _modulation(cc, mod_w, mod_b)
    xc = ctx
    for l in range(DEPTH):
        last = l == DEPTH - 1
        mod_x, mod_c = _mod_rows(mods[l], bsz)
        ffn_w = _ffn_weights(ffn_w_up[l], ffn_conv_w[l], ffn_conv_b[l], ffn_w_down[l])
        if l % 2 == 0:
            e = l // 2
            w_in = ab_w_in[e]
            o = 5 * HD
            w = {'a': w_in[:, :o].astype(BF16), 'qkv': w_in[:, o:o + QKV].astype(BF16),
                 'z': w_in[:, o + QKV:o + QKV + HD].astype(BF16),
                 'alpha': w_in[:, o + QKV + HD:o + QKV + HD + 2 * B_HEADS],
                 'beta': w_in[:, o + QKV + HD + 2 * B_HEADS:]}
            w_out = ab_w_out[e].astype(BF16)
            feats = []
            for seq, mod, per_batch in ((xc, mod_c, False), (x, mod_x, True)):
                feats.append(_ab_in(l, seq, mod, per_batch, norm_mix_g[l], w, hgrn_lb, gdn_conv_w[e],
                                    gdn_a_log[e], gdn_dt_bias[e]))
            outs = []
            sa = jnp.zeros((bsz, 2, A_HEADS, A_DV, A_DK), F32)
            sb = jnp.zeros((bsz, 2, B_HEADS, B_DV, B_DK), F32)
            for qa, ka, lfa, va, ga, qb, kb, vb, zb, gbc, gbr in feats:
                oa, sa = _hgrn_scan(qa, ka, lfa, va, sa)
                ob, sb = _gdn_scan(qb, kb, vb, gbc, gbr, sb)
                outs.append((oa, ob, ga, zb))
            oa, ob, ga, zb = outs[1]
            x = _ab_out(x, oa, ob, ga, zb, mod_x, True, hgrn_norm_g[e], gdn_norm_g[e], w_out)
            if not last:
                oa, ob, ga, zb = outs[0]
                xc = _ab_out(xc, oa, ob, ga, zb, mod_c, False, hgrn_norm_g[e], gdn_norm_g[e], w_out)
        else:
            o = l // 2
            wg = c_w_in[o][:, :C_WIDTH].astype(BF16)
            wx = c_w_in[o][:, C_WIDTH:].astype(BF16)
            wo = c_w_out[o].astype(BF16)
            gate_w = c_gate_w[o].astype(BF16)
            gate_b = c_gate_b[o].reshape(2, C_HEADS, 1, 2 * C_BLOCK)
            lam = c_lambda[o].reshape(2, 1, C_WIDTH)
            zero = jnp.zeros((bsz, 1, C_WIDTH), F32)

            def run(fwd, seq, mod, per_batch, h0, out_side=None):
                d = 0 if fwd else 1
                return _rglru_pass(fwd, seq, mod, per_batch, norm_mix_g[l], wx, c_conv_w[o], c_conv_b[o],
                                   gate_w[d], gate_b[d], lam[d], h0, out_side)

            hc_bwd, sb = run(False, xc, mod_c, False, zero)
            if last:
                (sf,) = run(True, xc, mod_c, False, zero)
            else:
                xc, sf = run(True, xc, mod_c, False, zero, (hc_bwd, wg, wo))
            h_bwd, _ = run(False, x, mod_x, True, sb)
            x, _ = run(True, x, mod_x, True, sf, (h_bwd, wg, wo))
        if not last:
            xc = _ffn(xc, mod_c, False, norm_ffn_g[l], ffn_w, latent=False)
        x = _ffn(x, mod_x, True, norm_ffn_g[l], ffn_w, latent=True, final_g=final_norm_g if last else None)
    return x
```

```python
import functools
import math

import numpy as np
import jax
import jax.numpy as jnp
from jax import lax
from jax.experimental import pallas as pl
from jax.experimental.pallas import tpu as pltpu

F32 = jnp.float32
BF16 = jnp.bfloat16

D_MODEL = 1024
DEPTH = 2
GRID_W = 64
EPS = 1e-6
A_HEADS = 4
A_DK = 128
A_DV = 128
B_HEADS = 4
B_DK = 128
B_DV = 128
B_CONV = 4
C_WIDTH = D_MODEL
C_HEADS = 4
C_BLOCK = C_WIDTH // C_HEADS
C_CONV = 4
C_GATE_C = 8.0
FFN_DIM = 2816
FFN_CONV = 3

HD = A_HEADS * A_DK
QKV = 2 * B_HEADS * B_DK + B_HEADS * B_DV

LANE = 128
SUBLANE = 8
VMEM_LIMIT = 56 * 1024 * 1024

TB = 256
AB_TM = 256
SUB = 16
LEVELS = (16, 32, 64, 128)
FT = 256
DOWN_PARTS = (6, 11)
FFN_TM = 1024
CONV_HALO = 8


def _cp(sem):
    return pltpu.CompilerParams(dimension_semantics=sem, vmem_limit_bytes=VMEM_LIMIT)


def _dot(a, b):
    return jnp.dot(a, b, preferred_element_type=F32)


def _dot_nt(a, b):
    return lax.dot_general(a, b, (((1,), (1,)), ((), ())), preferred_element_type=F32)


def _dot_tn(a, b):
    return lax.dot_general(a, b, (((0,), (0,)), ((), ())), preferred_element_type=F32)


def _sigmoid(x):
    return 0.5 * jnp.tanh(0.5 * x) + 0.5


def _silu(x):
    return x * _sigmoid(x)


def _softplus(x):
    return jnp.maximum(x, 0.0) + jnp.log1p(jnp.exp(-jnp.abs(x)))


def _gelu_tanh(x):
    c = math.sqrt(2.0 / math.pi)
    return 0.5 * x * (1.0 + jnp.tanh(c * (x + 0.044715 * (x * x * x))))


def _hi_lo(x):
    hi = x.astype(BF16)
    lo = (x - hi.astype(F32)).astype(BF16)
    return hi, lo


def _norm_mod(x, g, sc, sh):
    ms = jnp.mean(x * x, axis=-1, keepdims=True)
    return (x * lax.rsqrt(ms + EPS)) * g * (1.0 + sc) + sh


def _const_spec(shape):
    nd = len(shape)
    return pl.BlockSpec(shape, lambda *_: (0,) * nd, pipeline_mode=pl.Buffered(1))


MOD_TN = 1024


def _mod_kernel(c_ref, w_ref, b_ref, o_ref):
    s = _silu(c_ref[...])
    w = w_ref[0]
    s_hi, s_lo = _hi_lo(s)
    w_hi, w_lo = _hi_lo(w)
    acc = _dot(s_hi, w_hi) + _dot(s_hi, w_lo) + _dot(s_lo, w_hi)
    o_ref[0] = acc + b_ref[0]


def _modulation(cc, mod_w, mod_b):
    rows = cc.shape[0]
    n = mod_w.shape[-1]
    return pl.pallas_call(
        _mod_kernel,
        grid=(DEPTH, n // MOD_TN),
        in_specs=[pl.BlockSpec((rows, D_MODEL), lambda l, j: (0, 0)),
                  pl.BlockSpec((1, D_MODEL, MOD_TN), lambda l, j: (l, 0, j)),
                  pl.BlockSpec((1, 1, MOD_TN), lambda l, j: (l, 0, j))],
        out_specs=pl.BlockSpec((1, rows, MOD_TN), lambda l, j: (l, 0, j)),
        out_shape=jax.ShapeDtypeStruct((DEPTH, rows, n), F32),
        compiler_params=_cp(("arbitrary", "arbitrary")),
    )(cc, mod_w, mod_b.reshape(DEPTH, 1, n))


def _halo_specs(t_len, width, tm):
    per = tm // CONV_HALO
    last = t_len // CONV_HALO - 1
    main = pl.BlockSpec((1, tm, width), lambda b, i: (b, i, 0))
    prev = pl.BlockSpec((1, CONV_HALO, width), lambda b, i: (b, jnp.maximum(i * per - 1, 0), 0))
    nxt = pl.BlockSpec((1, CONV_HALO, width), lambda b, i: (b, jnp.minimum((i + 1) * per, last), 0))
    return main, prev, nxt


def _mod_spec(per_batch):
    if per_batch:
        return pl.BlockSpec((1, SUBLANE, D_MODEL), lambda b, *_: (b, 0, 0))
    return pl.BlockSpec((1, SUBLANE, D_MODEL), lambda b, *_: (0, 0, 0))


def _conv_tokens(ext_ref, w_ref, width, tm):
    left = width // 2
    acc = None
    for j in range(width):
        term = w_ref[j:j + 1, :] * ext_ref[pl.ds(CONV_HALO + j - left, tm), :]
        acc = term if acc is None else acc + term
    return acc


def _ab_in_kernel(layer, tm, x_ref, xp_ref, xn_ref, mod_ref, ng_ref, wa_ref, wqkv_ref, wz_ref, wab_ref, wabt_ref,
                  lbp_ref, convw_ref, coef_ref, coeft_ref,
                  qa_ref, ka_ref, lfa_ref, va_ref, ga_ref, qb_ref, kb_ref, vb_ref, zb_ref, gbc_ref, gbr_ref,
                  ext_ref):
    i = pl.program_id(1)
    nt = pl.num_programs(1)
    sh = mod_ref[0, 0:1, :]
    sc = mod_ref[0, 1:2, :]
    g = ng_ref[...]
    h_all = jnp.concatenate([_norm_mod(x_ref[0], g, sc, sh), _norm_mod(xp_ref[0], g, sc, sh),
                             _norm_mod(xn_ref[0], g, sc, sh)], axis=0).astype(BF16)
    hb = h_all[:tm]
    keep_p = (i > 0).astype(F32)
    keep_n = (i < nt - 1).astype(F32)

    def lower_bound(d):
        rws = [lbp_ref[d * (DEPTH + 1) + r:d * (DEPTH + 1) + r + 1, :] for r in range(DEPTH + 1)]
        mx = functools.reduce(jnp.maximum, rws)
        es = [jnp.exp(r - mx) for r in rws]
        return functools.reduce(jnp.add, es[:layer + 1]) / functools.reduce(jnp.add, es)

    def a_proj(group):
        return lambda: _dot(hb, wa_ref[:, group * HD:(group + 1) * HD])

    def put_q(p):
        qa_ref[0] = (_silu(p) * (A_DK ** -0.5)).astype(qa_ref.dtype)

    def put_forget(d):
        def put(p):
            lb_d = lower_bound(d)
            ka_ref[d, 0] = ((1.0 - lb_d) * _sigmoid(-p)).astype(ka_ref.dtype)
            lfa_ref[d, 0] = jnp.log(lb_d + (1.0 - lb_d) * _sigmoid(p))
        return put

    def put_v(p):
        va_ref[0] = p.astype(va_ref.dtype)

    def put_gate(p):
        ga_ref[0] = _silu(p).astype(ga_ref.dtype)

    def conv_proj(part):
        def run():
            proj = _dot(h_all, wqkv_ref[:, part * HD:(part + 1) * HD])
            slot = part % 2
            ext_ref[slot, 0:CONV_HALO, :] = proj[tm:tm + CONV_HALO] * keep_p
            ext_ref[slot, CONV_HALO:CONV_HALO + tm, :] = proj[:tm]
            ext_ref[slot, CONV_HALO + tm:, :] = proj[tm + CONV_HALO:] * keep_n
        return run

    def put_conv(part):
        def put(_):
            lo = part * HD
            t_all = _silu(_conv_tokens(ext_ref.at[part % 2], convw_ref.at[:, lo:lo + HD], B_CONV, tm))
            if part == 2:
                vb_ref[0] = t_all.astype(vb_ref.dtype)
                return
            ref, scale = ((qb_ref, B_DK ** -0.5), (kb_ref, 1.0))[part]
            for hh in range(B_HEADS):
                t = t_all[:, hh * B_DK:(hh + 1) * B_DK]
                ss = jnp.sum(t * t, axis=-1, keepdims=True)
                ref[0, :, hh * B_DK:(hh + 1) * B_DK] = (t * (lax.rsqrt(ss + EPS) * scale)).astype(ref.dtype)
        return put

    def put_z(p):
        zb_ref[0] = _silu(p).astype(zb_ref.dtype)

    def gate_proj():
        return _dot(hb, wab_ref[...]), _dot_nt(wabt_ref[...], hb)

    def put_gates(p):
        ab, abt = p
        lane = lax.broadcasted_iota(jnp.int32, ab.shape, 1)
        is_g = (lane % SUBLANE) < B_HEADS
        act = jnp.where(is_g, -jnp.exp(coef_ref[0:1, :]) * _softplus(ab + coef_ref[1:2, :]), _sigmoid(ab))
        row = lax.broadcasted_iota(jnp.int32, abt.shape, 0)
        is_g_t = (row % SUBLANE) < B_HEADS
        act_t = jnp.where(is_g_t, -jnp.exp(coeft_ref[:, 0:1]) * _softplus(abt + coeft_ref[:, 1:2]), _sigmoid(abt))
        col = lax.broadcasted_iota(jnp.int32, (tm, LANE), 1)
        for d in range(2):
            shifted = act if d == 0 else pltpu.roll(act, LANE - SUBLANE, axis=1)
            gbc_ref[0, d] = jnp.where(col < SUBLANE, shifted, 0.0)
            gbr_ref[0, d] = act_t[d * SUBLANE:(d + 1) * SUBLANE, :]

    stages = [(a_proj(0), put_q), (a_proj(1), put_forget(0)), (a_proj(2), put_forget(1)), (a_proj(3), put_v),
              (a_proj(4), put_gate), (conv_proj(0), put_conv(0)), (conv_proj(1), put_conv(1)),
              (conv_proj(2), put_conv(2)), (lambda: _dot(hb, wz_ref[...]), put_z), (gate_proj, put_gates)]
    pending = stages[0][0]()
    for s, (_, consume) in enumerate(stages):
        nxt = stages[s + 1][0]() if s + 1 < len(stages) else None
        consume(pending)
        pending = nxt


def _ab_in(layer, x, mod, per_batch, ng, w, hgrn_lb, conv_w, a_log, dt_bias):
    bsz, t_len, _ = x.shape
    tm = min(AB_TM, t_len)
    nt = t_len // tm
    main, prev, nxt = _halo_specs(t_len, D_MODEL, tm)
    tok = lambda width: pl.BlockSpec((1, tm, width), lambda b, i: (b, i, 0))
    tok2 = lambda width: pl.BlockSpec((2, 1, tm, width), lambda b, i: (0, b, i, 0))
    sds = lambda *s: jax.ShapeDtypeStruct(s, F32)
    half = lambda *s: jax.ShapeDtypeStruct(s, BF16)
    tok_half = half(bsz, t_len, HD)
    out_shape = (tok_half, half(2, bsz, t_len, HD), sds(2, bsz, t_len, HD), tok_half,
                 tok_half, tok_half, tok_half, tok_half,
                 tok_half, sds(bsz, 2, t_len, LANE), sds(bsz, 2, SUBLANE, t_len))
    out_specs = (tok(HD), tok2(HD), tok2(HD), tok(HD), tok(HD), tok(HD), tok(HD), tok(HD), tok(HD),
                 pl.BlockSpec((1, 2, tm, LANE), lambda b, i: (b, 0, i, 0)),
                 pl.BlockSpec((1, 2, SUBLANE, tm), lambda b, i: (b, 0, 0, i)))
    w_al, w_be = w['alpha'], w['beta']
    cols = [w_al[:, 0:4], w_be[:, 0:4], w_al[:, 4:8], w_be[:, 4:8]]
    wab = jnp.concatenate(cols, axis=1)
    wab_p = jnp.pad(wab, ((0, 0), (0, LANE - 2 * SUBLANE))).astype(BF16)
    wabt = wab.T.astype(BF16)
    zero4 = jnp.zeros((B_HEADS,), F32)
    a_flat = jnp.concatenate([a_log[0], zero4, a_log[1], zero4])
    d_flat = jnp.concatenate([dt_bias[0], zero4, dt_bias[1], zero4])
    coef = jnp.pad(jnp.stack([a_flat, d_flat]), ((0, 0), (0, LANE - 2 * SUBLANE)))
    coeft = jnp.stack([a_flat, d_flat], axis=1)
    hgrn_lb = hgrn_lb.reshape(2 * (DEPTH + 1), HD)
    args = (x, x, x, mod, ng.reshape(1, D_MODEL), w['a'], w['qkv'], w['z'], wab_p, wabt, hgrn_lb, conv_w, coef,
            coeft)
    in_specs = [main, prev, nxt, _mod_spec(per_batch), _const_spec((1, D_MODEL)), _const_spec(w['a'].shape),
                _const_spec(w['qkv'].shape), _const_spec(w['z'].shape), _const_spec(wab_p.shape),
                _const_spec(wabt.shape), _const_spec(hgrn_lb.shape), _const_spec(conv_w.shape),
                _const_spec(coef.shape), _const_spec(coeft.shape)]
    return pl.pallas_call(
        functools.partial(_ab_in_kernel, layer, tm),
        grid=(bsz, nt),
        in_specs=in_specs,
        out_specs=out_specs,
        out_shape=out_shape,
        scratch_shapes=[pltpu.VMEM((2, tm + 2 * CONV_HALO, HD), F32)],
        compiler_params=_cp(("arbitrary", "arbitrary")),
    )(*args)


def _both_dirs(stack):
    return np.stack([stack, stack[:, ::-1, ::-1]])


@functools.lru_cache(maxsize=None)
def _hgrn_consts():
    t = np.arange(TB)
    i, j = t[:, None], t[None, :]
    masks = [(i // SUB == j // SUB) & (j <= i)]
    for m in LEVELS:
        seg = 2 * m
        masks.append(((i // seg) == (j // seg)) & ((i % seg) >= m) & ((j % seg) < m))
    masks = _both_dirs(np.stack(masks).astype(np.float32))
    tri = _both_dirs((j <= i).astype(np.float32)[None])[:, 0]
    return jnp.asarray(tri, BF16), jnp.asarray(masks, F32)


@functools.lru_cache(maxsize=None)
def _gdn_consts():
    t = np.arange(TB)
    i, j = t[:, None], t[None, :]
    tri = _both_dirs(np.stack([j <= i, j < i]).astype(np.float32))
    blocks = [(i // SUB) == (j // SUB)]
    for m in LEVELS:
        seg = 2 * m
        blocks.append(((i // seg) == (j // seg)) & ((i // m) != (j // m)))
    blocks = np.stack(blocks).astype(np.float32)
    return jnp.asarray(tri[:, 0], BF16), jnp.asarray(tri, F32), jnp.asarray(blocks, F32)


def _time_block(d, i, nt):
    return i + d * (nt - 1 - 2 * i)


def _segment_exponents(xp, fwd):
    def row(r):
        return xp[r:r + 1, :]

    pieces = []
    for k in range(TB // SUB):
        rows = xp[k * SUB:(k + 1) * SUB]
        r = k * SUB - 1 if fwd else (k + 1) * SUB
        pieces.append(rows - row(r) if 0 <= r < TB else rows)
    out = [jnp.concatenate(pieces, axis=0)]
    for m in LEVELS:
        pieces = []
        for k in range(TB // (2 * m)):
            lo = k * 2 * m
            first, second = xp[lo:lo + m], xp[lo + m:lo + 2 * m]
            if fwd:
                c = row(lo + m - 1)
                pieces += [c - first, second - c]
            else:
                c = row(lo + m)
                pieces += [first - c, c - second]
        out.append(jnp.concatenate(pieces, axis=0))
    return out


def _hgrn_scan_kernel(q_ref, k_ref, lf_ref, v_ref, tri_ref, masks_ref, s0_ref, o_ref, sfin_ref, st_ref):
    d = pl.program_id(1)
    i = pl.program_id(2)
    nt = pl.num_programs(2)

    @pl.when(i == 0)
    def _():
        st_ref[...] = s0_ref[0, 0]

    n_lv = len(LEVELS)
    heads = range(A_HEADS)
    sls = [slice(hh * A_DK, (hh + 1) * A_DK) for hh in heads]

    def block(fwd):
        xps, tots = [], []
        for hh in heads:
            lf = lf_ref[0, 0, :, sls[hh]]
            lf_hi, lf_lo = _hi_lo(lf)
            xx = _dot(tri_ref[0], jnp.concatenate([lf_hi, lf_lo], axis=1))
            xps.append(xx[:, :A_DK] + xx[:, A_DK:])
            tots.append(jnp.sum(lf, axis=0, keepdims=True))
        exps = [_segment_exponents(xps[hh], fwd) for hh in heads]
        attns = []
        for hh in heads:
            x0 = exps[hh][0]
            raw = _dot_nt((q_ref[0, :, sls[hh]] * jnp.exp(x0)).astype(BF16),
                          (k_ref[0, 0, :, sls[hh]] * jnp.exp(-x0)).astype(BF16))
            attns.append(jnp.where(masks_ref[0, 0] > 0.0, raw, 0.0))
        for lv in range(1, n_lv + 1):
            for hh in heads:
                e = jnp.exp(exps[hh][lv])
                attns[hh] = attns[hh] + masks_ref[0, lv] * _dot_nt((q_ref[0, :, sls[hh]] * e).astype(BF16),
                                                                   (k_ref[0, 0, :, sls[hh]] * e).astype(BF16))
        sts = [st_ref[hh] for hh in heads]
        for hh in heads:
            e_in = jnp.exp(xps[hh])
            o = (_dot(attns[hh].astype(BF16), v_ref[0, :, sls[hh]].astype(BF16))
                 + _dot_nt((q_ref[0, :, sls[hh]] * e_in).astype(BF16), sts[hh].astype(BF16)))
            o_ref[0, 0, :, sls[hh]] = o.astype(o_ref.dtype)
        for hh in heads:
            e_out = jnp.exp(tots[hh] - xps[hh])
            st_ref[hh] = sts[hh] * jnp.exp(tots[hh]) + _dot_tn(v_ref[0, :, sls[hh]].astype(BF16),
                                                               (k_ref[0, 0, :, sls[hh]] * e_out).astype(BF16))

    @pl.when(d == 0)
    def _():
        block(True)

    @pl.when(d == 1)
    def _():
        block(False)

    @pl.when(i == nt - 1)
    def _():
        sfin_ref[0, 0] = st_ref[...]


def _hgrn_scan(q, k2, lf2, v, s0):
    bsz, t_len, _ = q.shape
    nt = t_len // TB
    tri, masks = _hgrn_consts()
    tok = pl.BlockSpec((1, TB, HD), lambda b, d, i: (b, _time_block(d, i, nt), 0))
    tok2 = pl.BlockSpec((1, 1, TB, HD), lambda b, d, i: (d, b, _time_block(d, i, nt), 0))
    st_spec = pl.BlockSpec((1, 1, A_HEADS, A_DV, A_DK), lambda b, d, i: (b, d, 0, 0, 0))
    return pl.pallas_call(
        _hgrn_scan_kernel,
        grid=(bsz, 2, nt),
        in_specs=[tok, tok2, tok2, tok,
                  pl.BlockSpec((1, TB, TB), lambda b, d, i: (d, 0, 0)),
                  pl.BlockSpec((1,) + masks.shape[1:], lambda b, d, i: (d, 0, 0, 0)),
                  st_spec],
        out_specs=(tok2, st_spec),
        out_shape=(jax.ShapeDtypeStruct((2, bsz, t_len, HD), BF16),
                   jax.ShapeDtypeStruct((bsz, 2, A_HEADS, A_DV, A_DK), F32)),
        scratch_shapes=[pltpu.VMEM((A_HEADS, A_DV, A_DK), F32)],
        compiler_params=_cp(("arbitrary", "arbitrary", "arbitrary")),
    )(q, k2, lf2, v, tri, masks, s0)


def _unit_tri_inverse_minus_identity(a_list, blocks_ref):
    b16 = lambda x: x.astype(BF16)
    eye = (lax.broadcasted_iota(jnp.int32, (TB, TB), 0) == lax.broadcasted_iota(jnp.int32, (TB, TB), 1))
    plus_eye = lambda e: b16(jnp.where(eye, 1.0, e))
    ps = [a * blocks_ref[0] for a in a_list]
    es = [-p for p in ps]
    width = 1
    while 2 * width < SUB:
        ps = [_dot(b16(p), b16(p)) for p in ps]
        es = [e + _dot(plus_eye(e), b16(p)) for e, p in zip(es, ps)]
        width *= 2
    for lv in range(len(LEVELS)):
        xs = [plus_eye(e) for e in es]
        ys = [_dot(x, b16(a * blocks_ref[lv + 1])) for x, a in zip(xs, a_list)]
        es = [e - _dot(b16(y), x) for e, y, x in zip(es, ys, xs)]
    return es


def _gdn_scan_kernel(q_ref, k_ref, v_ref, gbc_ref, gbr_ref, linc_ref, tri_ref, blocks_ref, s0_ref,
                     o_ref, sfin_ref, st_ref):
    i = pl.program_id(2)
    nt = pl.num_programs(2)

    @pl.when(i == 0)
    def _():
        st_ref[...] = s0_ref[0, 0]

    gcol = gbc_ref[0, 0]
    grow = gbr_ref[0, 0]
    linc = linc_ref[0]
    c_hi, c_lo = _hi_lo(gcol)
    gc_col = _dot(linc, jnp.concatenate([c_hi, c_lo], axis=1))
    gc_col = gc_col[:, :LANE] + gc_col[:, LANE:]
    r_hi, r_lo = _hi_lo(grow)
    gc_row = _dot_nt(jnp.concatenate([r_hi, r_lo], axis=0), linc)
    gc_row = gc_row[:SUBLANE] + gc_row[SUBLANE:]
    g_tot = jnp.sum(gcol, axis=0, keepdims=True)
    incl = tri_ref[0, 0] > 0.0
    strict = tri_ref[0, 1]

    heads = range(B_HEADS)
    sls = [slice(hh * B_DK, (hh + 1) * B_DK) for hh in heads]
    gccs = [gc_col[:, hh:hh + 1] for hh in heads]
    tots = [g_tot[:, hh:hh + 1] for hh in heads]
    betas = [gcol[:, B_HEADS + hh:B_HEADS + hh + 1] for hh in heads]
    a_list, attns, kbetas = [], [], []
    for hh in heads:
        q = q_ref[0, :, sls[hh]]
        k = k_ref[0, :, sls[hh]]
        gcr = gc_row[hh:hh + 1, :]
        decay = jnp.where(incl, jnp.exp(jnp.where(incl, gccs[hh] - gcr, 0.0)), 0.0)
        kbeta = k * betas[hh]
        prod = _dot_nt(jnp.concatenate([kbeta, q], axis=0).astype(BF16), k.astype(BF16))
        a_list.append(strict * prod[:TB] * decay)
        attns.append(prod[TB:] * decay)
        kbetas.append(kbeta)
    es = _unit_tri_inverse_minus_identity(a_list, blocks_ref)
    egs = [jnp.exp(gccs[hh]) for hh in heads]
    sols = []
    for hh in heads:
        rhs = jnp.concatenate([v_ref[0, :, sls[hh]] * betas[hh], kbetas[hh] * egs[hh]], axis=1)
        sols.append(rhs + _dot(es[hh].astype(BF16), rhs.astype(BF16)))
    sts = [st_ref[hh] for hh in heads]
    ws_qs = [_dot_nt(jnp.concatenate([sols[hh][:, B_DV:], q_ref[0, :, sls[hh]] * egs[hh]], axis=0).astype(BF16),
                     sts[hh].astype(BF16)) for hh in heads]
    v_news = [sols[hh][:, :B_DV] - ws_qs[hh][:TB] for hh in heads]
    for hh in heads:
        o = ws_qs[hh][TB:] + _dot(attns[hh].astype(BF16), v_news[hh].astype(BF16))
        o_ref[0, 0, :, sls[hh]] = o.astype(o_ref.dtype)
    for hh in heads:
        kd = k_ref[0, :, sls[hh]] * jnp.exp(tots[hh] - gccs[hh])
        st_ref[hh] = sts[hh] * jnp.exp(tots[hh]) + _dot_tn(v_news[hh].astype(BF16), kd.astype(BF16))

    @pl.when(i == nt - 1)
    def _():
        sfin_ref[0, 0] = st_ref[...]


def _gdn_scan(q, k, v, gbc, gbr, s0):
    bsz, t_len, _ = q.shape
    nt = t_len // TB
    linc, tri, blocks = _gdn_consts()
    tok = pl.BlockSpec((1, TB, HD), lambda b, d, i: (b, _time_block(d, i, nt), 0))
    tok2 = pl.BlockSpec((1, 1, TB, HD), lambda b, d, i: (d, b, _time_block(d, i, nt), 0))
    st_spec = pl.BlockSpec((1, 1, B_HEADS, B_DV, B_DK), lambda b, d, i: (b, d, 0, 0, 0))
    return pl.pallas_call(
        _gdn_scan_kernel,
        grid=(bsz, 2, nt),
        in_specs=[tok, tok, tok,
                  pl.BlockSpec((1, 1, TB, LANE), lambda b, d, i: (b, d, _time_block(d, i, nt), 0)),
                  pl.BlockSpec((1, 1, SUBLANE, TB), lambda b, d, i: (b, d, 0, _time_block(d, i, nt))),
                  pl.BlockSpec((1, TB, TB), lambda b, d, i: (d, 0, 0)),
                  pl.BlockSpec((1, 2, TB, TB), lambda b, d, i: (d, 0, 0, 0)),
                  _const_spec(blocks.shape),
                  st_spec],
        out_specs=(tok2, st_spec),
        out_shape=(jax.ShapeDtypeStruct((2, bsz, t_len, HD), BF16),
                   jax.ShapeDtypeStruct((bsz, 2, B_HEADS, B_DV, B_DK), F32)),
        scratch_shapes=[pltpu.VMEM((B_HEADS, B_DV, B_DK), F32)],
        compiler_params=_cp(("arbitrary", "arbitrary", "arbitrary")),
    )(q, k, v, gbc, gbr, linc, tri, blocks, s0)


def _head_norm_gate(o, g, gate, heads, width):
    parts = []
    for hh in range(heads):
        t = o[:, hh * width:(hh + 1) * width]
        ms = jnp.mean(t * t, axis=-1, keepdims=True)
        parts.append(t * lax.rsqrt(ms + EPS) * g)
    return jnp.concatenate(parts, axis=1) * gate


def _ab_out_kernel(x_ref, oaf_ref, oab_ref, obf_ref, obb_ref, ga_ref, zb_ref, mod_ref, ag_ref, bg_ref, w_ref,
                   o_ref):
    f32 = lambda ref, *idx: ref[idx].astype(F32)
    ya = _head_norm_gate(f32(oaf_ref, 0, 0) + f32(oab_ref, 0, 0), ag_ref[...], f32(ga_ref, 0), A_HEADS, A_DV)
    yb = _head_norm_gate(f32(obf_ref, 0, 0) + f32(obb_ref, 0, 0), bg_ref[...], f32(zb_ref, 0), B_HEADS, B_DV)
    y = _dot(jnp.concatenate([ya, yb], axis=1).astype(BF16), w_ref[...])
    o_ref[0] = x_ref[0] + mod_ref[0, 2:3, :] * y


def _ab_out(x, oa, ob, ga, zb, mod, per_batch, hgrn_g, gdn_g, w_out):
    bsz, t_len, _ = x.shape
    tok = lambda width: pl.BlockSpec((1, TB, width), lambda b, i: (b, i, 0))
    dir_spec = lambda d: pl.BlockSpec((1, 1, TB, HD), lambda b, i: (d, b, i, 0))
    return pl.pallas_call(
        _ab_out_kernel,
        grid=(bsz, t_len // TB),
        in_specs=[tok(D_MODEL), dir_spec(0), dir_spec(1), dir_spec(0), dir_spec(1), tok(HD), tok(HD),
                  _mod_spec(per_batch), _const_spec((1, A_DV)), _const_spec((1, B_DV)),
                  _const_spec(w_out.shape)],
        out_specs=tok(D_MODEL),
        out_shape=jax.ShapeDtypeStruct(x.shape, F32),
        compiler_params=_cp(("arbitrary", "arbitrary")),
    )(x, oa, oa, ob, ob, ga, zb, mod, hgrn_g.reshape(1, A_DV), gdn_g.reshape(1, B_DV), w_out)


def _ffn_kernel(tm, halo, row_w, final, x_ref, xp_ref, xn_ref, mod_ref, ng_ref, wa_ref, wv_ref, cw_ref, cb_ref,
                wd_ref, fg_ref, o_ref, h_ref, a_ref, g_ref):
    i = pl.program_id(1)
    nt = pl.num_programs(1)
    sh = mod_ref[0, 3:4, :]
    sc = mod_ref[0, 4:5, :]
    g = ng_ref[...]
    x = x_ref[0]
    h_ref[halo:halo + tm, :] = _norm_mod(x, g, sc, sh).astype(BF16)
    if halo:
        h_ref[0:halo, :] = _norm_mod(xp_ref[0], g, sc, sh).astype(BF16)
        h_ref[halo + tm:, :] = _norm_mod(xn_ref[0], g, sc, sh).astype(BF16)
        keep_p = (i > 0).astype(F32)
        keep_n = (i < nt - 1).astype(F32)
    pad = SUBLANE
    zeros = jnp.zeros((pad, FT), F32)
    for slot in range(2):
        a_ref[slot, 0:pad, :] = zeros
        a_ref[slot, pad + tm + 2 * halo:, :] = zeros
    col = lax.broadcasted_iota(jnp.int32, (tm, FT), 0) % row_w
    has_left = col >= 1
    has_right = col <= row_w - 2
    dys = (-1, 0, 1) if halo else (0,)
    n_tiles = FFN_DIM // FT

    def up(f):
        a = _dot(h_ref[...], wa_ref[f])
        slot = f % 2
        if halo:
            a_ref[slot, pad:pad + halo, :] = a[0:halo] * keep_p
            a_ref[slot, pad + halo:pad + halo + tm, :] = a[halo:halo + tm]
            a_ref[slot, pad + halo + tm:pad + tm + 2 * halo, :] = a[halo + tm:] * keep_n
        else:
            a_ref[slot, pad:pad + tm, :] = a
        return _dot(h_ref[halo:halo + tm, :], wv_ref[f])

    val = up(0)
    down = None
    for f in range(n_tiles):
        nxt_val = up(f + 1) if f + 1 < n_tiles else None
        slot = f % 2
        cw = cw_ref[f]
        cols = []
        for dx in (-1, 0, 1):
            s = None
            for dy in dys:
                tap = cw[(dy + 1) * FFN_CONV + dx + 1:(dy + 1) * FFN_CONV + dx + 2, :]
                term = tap * a_ref[slot, pl.ds(pad + halo + dy * row_w + dx, tm), :]
                s = term if s is None else s + term
            cols.append(s)
        conv = cols[1] + jnp.where(has_left, cols[0], 0.0) + jnp.where(has_right, cols[2], 0.0)
        g_ref[:, f * FT:(f + 1) * FT] = (_silu(conv + cb_ref[f]) * val).astype(BF16)
        val = nxt_val
        if f + 1 in DOWN_PARTS:
            lo = ([0] + list(DOWN_PARTS))[DOWN_PARTS.index(f + 1)] * FT
            part = _dot(g_ref[:, lo:(f + 1) * FT], wd_ref[lo:(f + 1) * FT, :])
            down = part if down is None else down + part
    y = x + mod_ref[0, 5:6, :] * down
    if final:
        ms = jnp.mean(y * y, axis=-1, keepdims=True)
        y = y * lax.rsqrt(ms + EPS) * fg_ref[...]
    o_ref[0] = y


def _ffn(x, mod, per_batch, ng, w, latent, final_g=None):
    bsz, t_len, _ = x.shape
    tm = FFN_TM if latent else t_len
    halo = GRID_W if latent else 0
    row_w = GRID_W if latent else t_len
    hb = max(halo, SUBLANE)
    per = tm // hb
    last = t_len // hb - 1
    main = pl.BlockSpec((1, tm, D_MODEL), lambda b, i: (b, i, 0))
    prev = pl.BlockSpec((1, hb, D_MODEL), lambda b, i: (b, jnp.maximum(i * per - 1, 0), 0))
    nxt = pl.BlockSpec((1, hb, D_MODEL), lambda b, i: (b, jnp.minimum((i + 1) * per, last), 0))
    final = final_g is not None
    fg = (final_g if final else jnp.ones((D_MODEL,), F32)).reshape(1, D_MODEL)
    kern = functools.partial(_ffn_kernel, tm, halo, row_w, final)
    return pl.pallas_call(
        kern,
        grid=(bsz, t_len // tm),
        in_specs=[main, prev, nxt, _mod_spec(per_batch), _const_spec((1, D_MODEL)),
                  _const_spec(w['a'].shape), _const_spec(w['v'].shape), _const_spec(w['cw'].shape),
                  _const_spec(w['cb'].shape), _const_spec(w['d'].shape), _const_spec((1, D_MODEL))],
        out_specs=main,
        out_shape=jax.ShapeDtypeStruct(x.shape, F32),
        scratch_shapes=[pltpu.VMEM((tm + 2 * halo, D_MODEL), BF16),
                        pltpu.VMEM((2, tm + 2 * halo + 2 * SUBLANE, FT), F32),
                        pltpu.VMEM((tm, FFN_DIM), BF16)],
        compiler_params=_cp(("arbitrary", "arbitrary")),
    )(x, x, x, mod, ng.reshape(1, D_MODEL), w['a'], w['v'], w['cw'], w['cb'], w['d'], fg)


def _ffn_weights(w_up, conv_w, conv_b, w_down):
    nf = FFN_DIM // FT
    tiles = lambda m: m.reshape(D_MODEL, nf, FT).transpose(1, 0, 2).astype(BF16)
    return {'a': tiles(w_up[:, :FFN_DIM]), 'v': tiles(w_up[:, FFN_DIM:]),
            'cw': conv_w.reshape(FFN_CONV * FFN_CONV, nf, FT).transpose(1, 0, 2),
            'cb': conv_b.reshape(nf, 1, FT),
            'd': w_down.astype(BF16)}


SCAN_CHUNKS = SUBLANE
SCAN_LEN = TB // SCAN_CHUNKS


def _chunk_carries(a_tot, u_tot, carry, fwd):
    row = lax.broadcasted_iota(jnp.int32, a_tot.shape, 0)
    a, u = a_tot, u_tot
    s = 1
    while s < SCAN_CHUNKS:
        has_prev = (row >= s) if fwd else (row < SCAN_CHUNKS - s)
        shift = s if fwd else SCAN_CHUNKS - s
        a_s = jnp.where(has_prev, pltpu.roll(a, shift, axis=0), 1.0)
        u_s = jnp.where(has_prev, pltpu.roll(u, shift, axis=0), 0.0)
        a, u = a * a_s, a * u_s + u
        s *= 2
    after = a * carry + u
    if fwd:
        entering = jnp.where(row >= 1, pltpu.roll(after, 1, axis=0), carry)
        return entering, after[SCAN_CHUNKS - 1:SCAN_CHUNKS, :]
    entering = jnp.where(row < SCAN_CHUNKS - 1, pltpu.roll(after, SCAN_CHUNKS - 1, axis=0), carry)
    return entering, after[0:1, :]


def _rglru_pass_kernel(fwd, with_out, *refs):
    (x_ref, xp_ref, xn_ref, mod_ref, ng_ref, wx_ref, cw_ref, cb_ref, gw_ref, gb_ref, lam_ref, perm_ref,
     h0_ref), refs = refs[:13], refs[13:]
    if fwd and with_out:
        (hb_ref, wg_ref, wo_ref, unperm_ref, o_ref), refs = refs[:5], refs[5:]
    elif not fwd:
        (o_ref,), refs = refs[:1], refs[1:]
    hfin_ref, ext_ref, a_ref, u_ref, hl_ref, p_ref, hs_ref, carry_ref = refs
    i = pl.program_id(1)
    nt = pl.num_programs(1)
    tb = i if fwd else nt - 1 - i

    @pl.when(i == 0)
    def _():
        carry_ref[...] = h0_ref[0]

    sh = mod_ref[0, 0:1, :]
    sc = mod_ref[0, 1:2, :]
    g = ng_ref[...]
    x = x_ref[0]
    h_all = jnp.concatenate([_dot(perm_ref[...], _norm_mod(x, g, sc, sh).astype(BF16)),
                             _norm_mod(xp_ref[0], g, sc, sh), _norm_mod(xn_ref[0], g, sc, sh)], axis=0).astype(BF16)
    keep_before = (tb > 0).astype(F32)
    keep_after = (tb < nt - 1).astype(F32)
    left = C_CONV // 2
    right = C_CONV - 1 - left
    row = lax.broadcasted_iota(jnp.int32, (SCAN_CHUNKS, C_BLOCK), 0)
    steps = list(range(SCAN_LEN)) if fwd else list(reversed(range(SCAN_LEN)))
    sls = [slice(hh * C_BLOCK, (hh + 1) * C_BLOCK) for hh in range(C_HEADS)]

    def project(hh):
        gate_pre = _dot(h_all[:TB], wg_ref[:, sls[hh]]) if fwd and with_out else None
        return _dot(h_all, wx_ref[:, sls[hh]]), gate_pre

    def features(hh, proj):
        sl = sls[hh]
        ext = ext_ref.at[hh % 2]
        xb = proj[:TB]
        before = proj[TB:TB + CONV_HALO] * keep_before
        after = proj[TB + CONV_HALO:] * keep_after
        ext[left * SCAN_CHUNKS:left * SCAN_CHUNKS + TB, :] = xb
        for s in range(1, left + 1):
            grp = xb[TB - s * SCAN_CHUNKS:TB - (s - 1) * SCAN_CHUNKS]
            halo = before[CONV_HALO - s:CONV_HALO - s + 1, :]
            ext[(left - s) * SCAN_CHUNKS:(left - s + 1) * SCAN_CHUNKS, :] = jnp.where(
                row == 0, halo, pltpu.roll(grp, 1, axis=0))
        for s in range(1, right + 1):
            grp = xb[(s - 1) * SCAN_CHUNKS:s * SCAN_CHUNKS]
            halo = after[s - 1:s, :]
            lo = (left + s - 1) * SCAN_CHUNKS + TB
            ext[lo:lo + SCAN_CHUNKS, :] = jnp.where(
                row == SCAN_CHUNKS - 1, halo, pltpu.roll(grp, SCAN_CHUNKS - 1, axis=0))
        xh = cb_ref[:, sl]
        for s in range(C_CONV):
            xh = xh + cw_ref[s:s + 1, sl] * ext[s * SCAN_CHUNKS:s * SCAN_CHUNKS + TB, :]
        gates = _sigmoid(_dot(xh.astype(BF16), gw_ref[hh]) + gb_ref[hh])
        a = jnp.exp(gates[:, :C_BLOCK] * ((-C_GATE_C) * _softplus(-lam_ref[:, sl])))
        a_ref[:, sl] = a
        u_ref[:, sl] = jnp.sqrt(1.0 - a * a) * (gates[:, C_BLOCK:] * xh)

    def scan(hh):
        sl = sls[hh]
        h_loc = p_loc = None
        for j in steps:
            rows = slice(j * SCAN_CHUNKS, (j + 1) * SCAN_CHUNKS)
            a_j = a_ref[rows, sl]
            u_j = u_ref[rows, sl]
            h_loc, p_loc = (u_j, a_j) if h_loc is None else (a_j * h_loc + u_j, a_j * p_loc)
            hl_ref[rows, sl] = h_loc
            p_ref[rows, sl] = p_loc
        entering, leaving = _chunk_carries(p_loc, h_loc, carry_ref[:, sl], fwd)
        carry_ref[:, sl] = leaving
        if fwd and not with_out:
            return
        for j in steps:
            rows = slice(j * SCAN_CHUNKS, (j + 1) * SCAN_CHUNKS)
            h_j = hl_ref[rows, sl] + p_ref[rows, sl] * entering
            if fwd:
                hs_ref[rows, sl] = h_j + hb_ref[0, rows, sl]
            else:
                o_ref[0, rows, sl] = h_j

    def gated_out(hh, gate_pre):
        sl = sls[hh]
        z = _dot(unperm_ref[...], (_gelu_tanh(gate_pre) * hs_ref[:, sl]).astype(BF16)).astype(BF16)
        return _dot(z, wo_ref[sl, :])

    projs = [project(0)]
    y = None
    for hh in range(C_HEADS):
        if hh + 1 < C_HEADS:
            projs.append(project(hh + 1))
        features(hh, projs[hh][0])
        if fwd and with_out and hh > 0:
            part = gated_out(hh - 1, projs[hh - 1][1])
            y = part if y is None else y + part
        scan(hh)
    if fwd and with_out:
        y = y + gated_out(C_HEADS - 1, projs[C_HEADS - 1][1])

    @pl.when(i == nt - 1)
    def _():
        hfin_ref[0] = carry_ref[...]

    if fwd and with_out:
        o_ref[0] = x + mod_ref[0, 2:3, :] * y


@functools.lru_cache(maxsize=None)
def _scan_order():
    t = np.arange(TB)
    perm = np.zeros((TB, TB), np.float32)
    perm[(t % SCAN_LEN) * SCAN_CHUNKS + t // SCAN_LEN, t] = 1.0
    return jnp.asarray(perm, BF16), jnp.asarray(perm.T, BF16)


def _rglru_pass(fwd, x, mod, per_batch, ng, wx, conv_w, conv_b, gate_w, gate_b, lam, h0, out_side=None):
    bsz, t_len, _ = x.shape
    nt = t_len // TB
    per = TB // CONV_HALO
    last = t_len // CONV_HALO - 1
    tbi = (lambda i: i) if fwd else (lambda i: nt - 1 - i)
    main = pl.BlockSpec((1, TB, D_MODEL), lambda b, i: (b, tbi(i), 0))
    prev = pl.BlockSpec((1, CONV_HALO, D_MODEL), lambda b, i: (b, jnp.maximum(tbi(i) * per - 1, 0), 0))
    nxt = pl.BlockSpec((1, CONV_HALO, D_MODEL), lambda b, i: (b, jnp.minimum((tbi(i) + 1) * per, last), 0))
    st_spec = pl.BlockSpec((1, 1, C_WIDTH), lambda b, i: (b, 0, 0))
    perm, unperm = _scan_order()
    args = [x, x, x, mod, ng.reshape(1, D_MODEL), wx, conv_w, conv_b.reshape(1, C_WIDTH), gate_w, gate_b, lam, perm,
            h0]
    in_specs = [main, prev, nxt, _mod_spec(per_batch), _const_spec((1, D_MODEL)), _const_spec(wx.shape),
                _const_spec(conv_w.shape), _const_spec((1, C_WIDTH)), _const_spec(gate_w.shape),
                _const_spec(gate_b.shape), _const_spec(lam.shape), _const_spec(perm.shape), st_spec]
    st_shape = jax.ShapeDtypeStruct((bsz, 1, C_WIDTH), F32)
    tok_shape = jax.ShapeDtypeStruct((bsz, t_len, C_WIDTH), F32)
    with_out = out_side is not None
    if fwd and with_out:
        h_bwd, wg, wo = out_side
        args += [h_bwd, wg, wo, unperm]
        in_specs += [main, _const_spec(wg.shape), _const_spec(wo.shape), _const_spec(unperm.shape)]
    if fwd and not with_out:
        out_specs, out_shape = (st_spec,), (st_shape,)
    else:
        out_specs, out_shape = (main, st_spec), (tok_shape, st_shape)
    blk = lambda: pltpu.VMEM((TB, C_WIDTH), F32)
    return pl.pallas_call(
        functools.partial(_rglru_pass_kernel, fwd, with_out),
        grid=(bsz, nt),
        in_specs=in_specs,
        out_specs=out_specs,
        out_shape=out_shape,
        scratch_shapes=[pltpu.VMEM((2, TB + (C_CONV - 1) * SCAN_CHUNKS, C_BLOCK), F32), blk(), blk(), blk(), blk(),
                        blk(),
                        pltpu.VMEM((1, C_WIDTH), F32)],
        compiler_params=_cp(("arbitrary", "arbitrary")),
    )(*args)


def _mod_rows(mods_l, bsz):
    m = mods_l.reshape(mods_l.shape[0], 6, D_MODEL)
    m = jnp.pad(m, ((0, 0), (0, SUBLANE - 6), (0, 0)))
    return m[:bsz], m[bsz:bsz + 1]


def kernel(x, c, ctx, c_ctx, mod_w, mod_b, norm_mix_g, norm_ffn_g, final_norm_g, ab_w_in, hgrn_lb, hgrn_norm_g, gdn_conv_w, gdn_a_log, gdn_dt_bias, gdn_norm_g, ab_w_out, c_w_in, c_conv_w, c_conv_b, c_gate_w, c_gate_b, c_lambda, c_w_out, ffn_w_up, ffn_conv_w, ffn_conv_b, ffn_w_down):
    bsz = x.shape[0]
    rows = 2 * SUBLANE
    cc = jnp.concatenate([c, c_ctx[None, :], jnp.zeros((rows - bsz - 1, D_MODEL), F32)], axis=0)
    mods = _modulation(cc, mod_w, mod_b)
    xc = ctx
    for l in range(DEPTH):
        last = l == DEPTH - 1
        mod_x, mod_c = _mod_rows(mods[l], bsz)
        ffn_w = _ffn_weights(ffn_w_up[l], ffn_conv_w[l], ffn_conv_b[l], ffn_w_down[l])
        if l % 2 == 0:
            e = l // 2
            w_in = ab_w_in[e]
            o = 5 * HD
            w = {'a': w_in[:, :o].astype(BF16), 'qkv': w_in[:, o:o + QKV].astype(BF16),
                 'z': w_in[:, o + QKV:o + QKV + HD].astype(BF16),
                 'alpha': w_in[:, o + QKV + HD:o + QKV + HD + 2 * B_HEADS],
                 'beta': w_in[:, o + QKV + HD + 2 * B_HEADS:]}
            w_out = ab_w_out[e].astype(BF16)
            feats = []
            for seq, mod, per_batch in ((xc, mod_c, False), (x, mod_x, True)):
                feats.append(_ab_in(l, seq, mod, per_batch, norm_mix_g[l], w, hgrn_lb, gdn_conv_w[e],
                                    gdn_a_log[e], gdn_dt_bias[e]))
            outs = []
            sa = jnp.zeros((bsz, 2, A_HEADS, A_DV, A_DK), F32)
            sb = jnp.zeros((bsz, 2, B_HEADS, B_DV, B_DK), F32)
            for qa, ka, lfa, va, ga, qb, kb, vb, zb, gbc, gbr in feats:
                oa, sa = _hgrn_scan(qa, ka, lfa, va, sa)
                ob, sb = _gdn_scan(qb, kb, vb, gbc, gbr, sb)
                outs.append((oa, ob, ga, zb))
            oa, ob, ga, zb = outs[1]
            x = _ab_out(x, oa, ob, ga, zb, mod_x, True, hgrn_norm_g[e], gdn_norm_g[e], w_out)
            if not last:
                oa, ob, ga, zb = outs[0]
                xc = _ab_out(xc, oa, ob, ga, zb, mod_c, False, hgrn_norm_g[e], gdn_norm_g[e], w_out)
        else:
            o = l // 2
            wg = c_w_in[o][:, :C_WIDTH].astype(BF16)
            wx = c_w_in[o][:, C_WIDTH:].astype(BF16)
            wo = c_w_out[o].astype(BF16)
            gate_w = c_gate_w[o].astype(BF16)
            gate_b = c_gate_b[o].reshape(2, C_HEADS, 1, 2 * C_BLOCK)
            lam = c_lambda[o].reshape(2, 1, C_WIDTH)
            zero = jnp.zeros((bsz, 1, C_WIDTH), F32)

            def run(fwd, seq, mod, per_batch, h0, out_side=None):
                d = 0 if fwd else 1
                return _rglru_pass(fwd, seq, mod, per_batch, norm_mix_g[l], wx, c_conv_w[o], c_conv_b[o],
                                   gate_w[d], gate_b[d], lam[d], h0, out_side)

            hc_bwd, sb = run(False, xc, mod_c, False, zero)
            if last:
                (sf,) = run(True, xc, mod_c, False, zero)
            else:
                xc, sf = run(True, xc, mod_c, False, zero, (hc_bwd, wg, wo))
            h_bwd, _ = run(False, x, mod_x, True, sb)
            x, _ = run(True, x, mod_x, True, sf, (h_bwd, wg, wo))
        if not last:
            xc = _ffn(xc, mod_c, False, norm_ffn_g[l], ffn_w, latent=False)
        x = _ffn(x, mod_x, True, norm_ffn_g[l], ffn_w, latent=True, final_g=final_norm_g if last else None)
    return x
```

```python
import functools
import math

import numpy as np
import jax
import jax.numpy as jnp
from jax import lax
from jax.experimental import pallas as pl
from jax.experimental.pallas import tpu as pltpu

F32 = jnp.float32
BF16 = jnp.bfloat16

D_MODEL = 1024
DEPTH = 2
GRID_W = 64
EPS = 1e-6
A_HEADS = 4
A_DK = 128
A_DV = 128
B_HEADS = 4
B_DK = 128
B_DV = 128
B_CONV = 4
C_WIDTH = D_MODEL
C_HEADS = 4
C_BLOCK = C_WIDTH // C_HEADS
C_CONV = 4
C_GATE_C = 8.0
FFN_DIM = 2816
FFN_CONV = 3

HD = A_HEADS * A_DK
QKV = 2 * B_HEADS * B_DK + B_HEADS * B_DV

LANE = 128
SUBLANE = 8
VMEM_LIMIT = 56 * 1024 * 1024

TB = 256
AB_TM = 256
AB_OUT_TM = 1024
AB_AHEAD = 1
SUB = 16
LEVELS = (16, 32, 64, 128)
FT = 256
DOWN_PARTS = (6, 11)
FFN_TM = 1024
CONV_HALO = 8


def _cp(sem):
    return pltpu.CompilerParams(dimension_semantics=sem, vmem_limit_bytes=VMEM_LIMIT)


def _dot(a, b):
    return jnp.dot(a, b, preferred_element_type=F32)


def _dot_nt(a, b):
    return lax.dot_general(a, b, (((1,), (1,)), ((), ())), preferred_element_type=F32)


def _dot_tn(a, b):
    return lax.dot_general(a, b, (((0,), (0,)), ((), ())), preferred_element_type=F32)


def _sigmoid(x):
    return 0.5 * jnp.tanh(0.5 * x) + 0.5


def _silu(x):
    return x * _sigmoid(x)


def _softplus(x):
    return jnp.maximum(x, 0.0) + jnp.log1p(jnp.exp(-jnp.abs(x)))


def _gelu_tanh(x):
    c = math.sqrt(2.0 / math.pi)
    return 0.5 * x * (1.0 + jnp.tanh(c * (x + 0.044715 * (x * x * x))))


def _hi_lo(x):
    hi = x.astype(BF16)
    lo = (x - hi.astype(F32)).astype(BF16)
    return hi, lo


def _norm_mod(x, g, sc, sh):
    ms = jnp.mean(x * x, axis=-1, keepdims=True)
    return (x * lax.rsqrt(ms + EPS)) * g * (1.0 + sc) + sh


def _const_spec(shape):
    nd = len(shape)
    return pl.BlockSpec(shape, lambda *_: (0,) * nd, pipeline_mode=pl.Buffered(1))


MOD_TN = 1024


def _mod_kernel(c_ref, w_ref, b_ref, o_ref):
    s = _silu(c_ref[...])
    w = w_ref[0]
    s_hi, s_lo = _hi_lo(s)
    w_hi, w_lo = _hi_lo(w)
    acc = _dot(s_hi, w_hi) + _dot(s_hi, w_lo) + _dot(s_lo, w_hi)
    o_ref[0] = acc + b_ref[0]


def _modulation(cc, mod_w, mod_b):
    rows = cc.shape[0]
    n = mod_w.shape[-1]
    return pl.pallas_call(
        _mod_kernel,
        grid=(DEPTH, n // MOD_TN),
        in_specs=[pl.BlockSpec((rows, D_MODEL), lambda l, j: (0, 0)),
                  pl.BlockSpec((1, D_MODEL, MOD_TN), lambda l, j: (l, 0, j)),
                  pl.BlockSpec((1, 1, MOD_TN), lambda l, j: (l, 0, j))],
        out_specs=pl.BlockSpec((1, rows, MOD_TN), lambda l, j: (l, 0, j)),
        out_shape=jax.ShapeDtypeStruct((DEPTH, rows, n), F32),
        compiler_params=_cp(("arbitrary", "arbitrary")),
    )(cc, mod_w, mod_b.reshape(DEPTH, 1, n))


def _halo_specs(t_len, width, tm):
    per = tm // CONV_HALO
    last = t_len // CONV_HALO - 1
    main = pl.BlockSpec((1, tm, width), lambda b, i: (b, i, 0))
    prev = pl.BlockSpec((1, CONV_HALO, width), lambda b, i: (b, jnp.maximum(i * per - 1, 0), 0))
    nxt = pl.BlockSpec((1, CONV_HALO, width), lambda b, i: (b, jnp.minimum((i + 1) * per, last), 0))
    return main, prev, nxt


def _mod_spec(per_batch):
    if per_batch:
        return pl.BlockSpec((1, SUBLANE, D_MODEL), lambda b, *_: (b, 0, 0))
    return pl.BlockSpec((1, SUBLANE, D_MODEL), lambda b, *_: (0, 0, 0))


def _conv_tokens(ext_ref, w_ref, width, tm):
    left = width // 2
    acc = None
    for j in range(width):
        term = w_ref[j:j + 1, :] * ext_ref[pl.ds(CONV_HALO + j - left, tm), :]
        acc = term if acc is None else acc + term
    return acc


def _ab_in_kernel(layer, tm, x_ref, xp_ref, xn_ref, mod_ref, ng_ref, wa_ref, wqkv_ref, wz_ref, wab_ref, wabt_ref,
                  lbp_ref, convw_ref, coef_ref, coeft_ref,
                  qa_ref, ka_ref, lfa_ref, va_ref, ga_ref, qb_ref, kb_ref, vb_ref, zb_ref, gbc_ref, gbr_ref,
                  ext_ref):
    i = pl.program_id(1)
    nt = pl.num_programs(1)
    sh = mod_ref[0, 0:1, :]
    sc = mod_ref[0, 1:2, :]
    g = ng_ref[...]
    h_all = jnp.concatenate([_norm_mod(x_ref[0], g, sc, sh), _norm_mod(xp_ref[0], g, sc, sh),
                             _norm_mod(xn_ref[0], g, sc, sh)], axis=0).astype(BF16)
    hb = h_all[:tm]
    keep_p = (i > 0).astype(F32)
    keep_n = (i < nt - 1).astype(F32)

    def lower_bound(d):
        rws = [lbp_ref[d * (DEPTH + 1) + r:d * (DEPTH + 1) + r + 1, :] for r in range(DEPTH + 1)]
        mx = functools.reduce(jnp.maximum, rws)
        es = [jnp.exp(r - mx) for r in rws]
        return functools.reduce(jnp.add, es[:layer + 1]) / functools.reduce(jnp.add, es)

    def a_proj(group):
        return lambda: _dot(hb, wa_ref[:, group * HD:(group + 1) * HD])

    def put_q(p):
        qa_ref[0] = (_silu(p) * (A_DK ** -0.5)).astype(qa_ref.dtype)

    def put_forget(d):
        def put(p):
            lb_d = lower_bound(d)
            ka_ref[d, 0] = ((1.0 - lb_d) * _sigmoid(-p)).astype(ka_ref.dtype)
            lfa_ref[d, 0] = jnp.log(lb_d + (1.0 - lb_d) * _sigmoid(p))
        return put

    def put_v(p):
        va_ref[0] = p.astype(va_ref.dtype)

    def put_gate(p):
        ga_ref[0] = _silu(p).astype(ga_ref.dtype)

    def conv_proj(part):
        def run():
            proj = _dot(h_all, wqkv_ref[:, part * HD:(part + 1) * HD])
            slot = part % (AB_AHEAD + 1)
            ext_ref[slot, 0:CONV_HALO, :] = proj[tm:tm + CONV_HALO] * keep_p
            ext_ref[slot, CONV_HALO:CONV_HALO + tm, :] = proj[:tm]
            ext_ref[slot, CONV_HALO + tm:, :] = proj[tm + CONV_HALO:] * keep_n
        return run

    def put_conv(part):
        def put(_):
            lo = part * HD
            t_all = _silu(_conv_tokens(ext_ref.at[part % (AB_AHEAD + 1)], convw_ref.at[:, lo:lo + HD], B_CONV, tm))
            if part == 2:
                vb_ref[0] = t_all.astype(vb_ref.dtype)
                return
            ref, scale = ((qb_ref, B_DK ** -0.5), (kb_ref, 1.0))[part]
            for hh in range(B_HEADS):
                t = t_all[:, hh * B_DK:(hh + 1) * B_DK]
                ss = jnp.sum(t * t, axis=-1, keepdims=True)
                ref[0, :, hh * B_DK:(hh + 1) * B_DK] = (t * (lax.rsqrt(ss + EPS) * scale)).astype(ref.dtype)
        return put

    def put_z(p):
        zb_ref[0] = _silu(p).astype(zb_ref.dtype)

    def gate_proj():
        return _dot(hb, wab_ref[...]), _dot_nt(wabt_ref[...], hb)

    def put_gates(p):
        ab, abt = p
        lane = lax.broadcasted_iota(jnp.int32, ab.shape, 1)
        is_g = (lane % SUBLANE) < B_HEADS
        act = jnp.where(is_g, -jnp.exp(coef_ref[0:1, :]) * _softplus(ab + coef_ref[1:2, :]), _sigmoid(ab))
        row = lax.broadcasted_iota(jnp.int32, abt.shape, 0)
        is_g_t = (row % SUBLANE) < B_HEADS
        act_t = jnp.where(is_g_t, -jnp.exp(coeft_ref[:, 0:1]) * _softplus(abt + coeft_ref[:, 1:2]), _sigmoid(abt))
        col = lax.broadcasted_iota(jnp.int32, (tm, LANE), 1)
        for d in range(2):
            shifted = act if d == 0 else pltpu.roll(act, LANE - SUBLANE, axis=1)
            gbc_ref[0, d] = jnp.where(col < SUBLANE, shifted, 0.0)
            gbr_ref[0, d] = act_t[d * SUBLANE:(d + 1) * SUBLANE, :]

    stages = [(a_proj(0), put_q), (a_proj(1), put_forget(0)), (a_proj(2), put_forget(1)), (a_proj(3), put_v),
              (a_proj(4), put_gate), (conv_proj(0), put_conv(0)), (conv_proj(1), put_conv(1)),
              (conv_proj(2), put_conv(2)), (lambda: _dot(hb, wz_ref[...]), put_z), (gate_proj, put_gates)]
    results = [stages[s][0]() for s in range(AB_AHEAD)]
    for s, (_, consume) in enumerate(stages):
        if s + AB_AHEAD < len(stages):
            results.append(stages[s + AB_AHEAD][0]())
        consume(results[s])


def _ab_in(layer, x, mod, per_batch, ng, w, hgrn_lb, conv_w, a_log, dt_bias):
    bsz, t_len, _ = x.shape
    tm = min(AB_TM, t_len)
    nt = t_len // tm
    main, prev, nxt = _halo_specs(t_len, D_MODEL, tm)
    tok = lambda width: pl.BlockSpec((1, tm, width), lambda b, i: (b, i, 0))
    tok2 = lambda width: pl.BlockSpec((2, 1, tm, width), lambda b, i: (0, b, i, 0))
    sds = lambda *s: jax.ShapeDtypeStruct(s, F32)
    half = lambda *s: jax.ShapeDtypeStruct(s, BF16)
    tok_half = half(bsz, t_len, HD)
    out_shape = (tok_half, half(2, bsz, t_len, HD), sds(2, bsz, t_len, HD), tok_half,
                 tok_half, tok_half, tok_half, tok_half,
                 tok_half, sds(bsz, 2, t_len, LANE), sds(bsz, 2, SUBLANE, t_len))
    out_specs = (tok(HD), tok2(HD), tok2(HD), tok(HD), tok(HD), tok(HD), tok(HD), tok(HD), tok(HD),
                 pl.BlockSpec((1, 2, tm, LANE), lambda b, i: (b, 0, i, 0)),
                 pl.BlockSpec((1, 2, SUBLANE, tm), lambda b, i: (b, 0, 0, i)))
    w_al, w_be = w['alpha'], w['beta']
    cols = [w_al[:, 0:4], w_be[:, 0:4], w_al[:, 4:8], w_be[:, 4:8]]
    wab = jnp.concatenate(cols, axis=1)
    wab_p = jnp.pad(wab, ((0, 0), (0, LANE - 2 * SUBLANE))).astype(BF16)
    wabt = wab.T.astype(BF16)
    zero4 = jnp.zeros((B_HEADS,), F32)
    a_flat = jnp.concatenate([a_log[0], zero4, a_log[1], zero4])
    d_flat = jnp.concatenate([dt_bias[0], zero4, dt_bias[1], zero4])
    coef = jnp.pad(jnp.stack([a_flat, d_flat]), ((0, 0), (0, LANE - 2 * SUBLANE)))
    coeft = jnp.stack([a_flat, d_flat], axis=1)
    hgrn_lb = hgrn_lb.reshape(2 * (DEPTH + 1), HD)
    args = (x, x, x, mod, ng.reshape(1, D_MODEL), w['a'], w['qkv'], w['z'], wab_p, wabt, hgrn_lb, conv_w, coef,
            coeft)
    in_specs = [main, prev, nxt, _mod_spec(per_batch), _const_spec((1, D_MODEL)), _const_spec(w['a'].shape),
                _const_spec(w['qkv'].shape), _const_spec(w['z'].shape), _const_spec(wab_p.shape),
                _const_spec(wabt.shape), _const_spec(hgrn_lb.shape), _const_spec(conv_w.shape),
                _const_spec(coef.shape), _const_spec(coeft.shape)]
    return pl.pallas_call(
        functools.partial(_ab_in_kernel, layer, tm),
        grid=(bsz, nt),
        in_specs=in_specs,
        out_specs=out_specs,
        out_shape=out_shape,
        scratch_shapes=[pltpu.VMEM((AB_AHEAD + 1, tm + 2 * CONV_HALO, HD), F32)],
        compiler_params=_cp(("arbitrary", "arbitrary")),
    )(*args)


def _both_dirs(stack):
    return np.stack([stack, stack[:, ::-1, ::-1]])


@functools.lru_cache(maxsize=None)
def _hgrn_consts():
    t = np.arange(TB)
    i, j = t[:, None], t[None, :]
    masks = [(i // SUB == j // SUB) & (j <= i)]
    for m in LEVELS:
        seg = 2 * m
        masks.append(((i // seg) == (j // seg)) & ((i % seg) >= m) & ((j % seg) < m))
    masks = _both_dirs(np.stack(masks).astype(np.float32))
    tri = _both_dirs((j <= i).astype(np.float32)[None])[:, 0]
    return jnp.asarray(tri, BF16), jnp.asarray(masks, F32)


@functools.lru_cache(maxsize=None)
def _gdn_consts():
    t = np.arange(TB)
    i, j = t[:, None], t[None, :]
    tri = _both_dirs(np.stack([j <= i, j < i]).astype(np.float32))
    blocks = [(i // SUB) == (j // SUB)]
    for m in LEVELS:
        seg = 2 * m
        blocks.append(((i // seg) == (j // seg)) & ((i // m) != (j // m)))
    blocks = np.stack(blocks).astype(np.float32)
    return jnp.asarray(tri[:, 0], BF16), jnp.asarray(tri, F32), jnp.asarray(blocks, F32)


def _time_block(d, i, nt):
    return i + d * (nt - 1 - 2 * i)


def _segment_exponents(xp, fwd):
    def row(r):
        return xp[r:r + 1, :]

    pieces = []
    for k in range(TB // SUB):
        rows = xp[k * SUB:(k + 1) * SUB]
        r = k * SUB - 1 if fwd else (k + 1) * SUB
        pieces.append(rows - row(r) if 0 <= r < TB else rows)
    out = [jnp.concatenate(pieces, axis=0)]
    for m in LEVELS:
        pieces = []
        for k in range(TB // (2 * m)):
            lo = k * 2 * m
            first, second = xp[lo:lo + m], xp[lo + m:lo + 2 * m]
            if fwd:
                c = row(lo + m - 1)
                pieces += [c - first, second - c]
            else:
                c = row(lo + m)
                pieces += [first - c, c - second]
        out.append(jnp.concatenate(pieces, axis=0))
    return out


def _hgrn_scan_kernel(q_ref, k_ref, lf_ref, v_ref, tri_ref, masks_ref, s0_ref, o_ref, sfin_ref, st_ref):
    d = pl.program_id(1)
    i = pl.program_id(2)
    nt = pl.num_programs(2)

    @pl.when(i == 0)
    def _():
        st_ref[...] = s0_ref[0, 0]

    n_lv = len(LEVELS)
    heads = range(A_HEADS)
    sls = [slice(hh * A_DK, (hh + 1) * A_DK) for hh in heads]

    def block(fwd):
        xps, tots = [], []
        for hh in heads:
            lf = lf_ref[0, 0, :, sls[hh]]
            lf_hi, lf_lo = _hi_lo(lf)
            xx = _dot(tri_ref[0], jnp.concatenate([lf_hi, lf_lo], axis=1))
            xps.append(xx[:, :A_DK] + xx[:, A_DK:])
            tots.append(jnp.sum(lf, axis=0, keepdims=True))
        exps = [_segment_exponents(xps[hh], fwd) for hh in heads]
        attns = []
        for hh in heads:
            x0 = exps[hh][0]
            raw = _dot_nt((q_ref[0, :, sls[hh]] * jnp.exp(x0)).astype(BF16),
                          (k_ref[0, 0, :, sls[hh]] * jnp.exp(-x0)).astype(BF16))
            attns.append(jnp.where(masks_ref[0, 0] > 0.0, raw, 0.0))
        for lv in range(1, n_lv + 1):
            for hh in heads:
                e = jnp.exp(exps[hh][lv])
                attns[hh] = attns[hh] + masks_ref[0, lv] * _dot_nt((q_ref[0, :, sls[hh]] * e).astype(BF16),
                                                                   (k_ref[0, 0, :, sls[hh]] * e).astype(BF16))
        sts = [st_ref[hh] for hh in heads]
        for hh in heads:
            e_in = jnp.exp(xps[hh])
            o = (_dot(attns[hh].astype(BF16), v_ref[0, :, sls[hh]].astype(BF16))
                 + _dot_nt((q_ref[0, :, sls[hh]] * e_in).astype(BF16), sts[hh].astype(BF16)))
            o_ref[0, 0, :, sls[hh]] = o.astype(o_ref.dtype)
        for hh in heads:
            e_out = jnp.exp(tots[hh] - xps[hh])
            st_ref[hh] = sts[hh] * jnp.exp(tots[hh]) + _dot_tn(v_ref[0, :, sls[hh]].astype(BF16),
                                                               (k_ref[0, 0, :, sls[hh]] * e_out).astype(BF16))

    @pl.when(d == 0)
    def _():
        block(True)

    @pl.when(d == 1)
    def _():
        block(False)

    @pl.when(i == nt - 1)
    def _():
        sfin_ref[0, 0] = st_ref[...]


def _hgrn_scan(q, k2, lf2, v, s0):
    bsz, t_len, _ = q.shape
    nt = t_len // TB
    tri, masks = _hgrn_consts()
    tok = pl.BlockSpec((1, TB, HD), lambda b, d, i: (b, _time_block(d, i, nt), 0))
    tok2 = pl.BlockSpec((1, 1, TB, HD), lambda b, d, i: (d, b, _time_block(d, i, nt), 0))
    st_spec = pl.BlockSpec((1, 1, A_HEADS, A_DV, A_DK), lambda b, d, i: (b, d, 0, 0, 0))
    return pl.pallas_call(
        _hgrn_scan_kernel,
        grid=(bsz, 2, nt),
        in_specs=[tok, tok2, tok2, tok,
                  pl.BlockSpec((1, TB, TB), lambda b, d, i: (d, 0, 0)),
                  pl.BlockSpec((1,) + masks.shape[1:], lambda b, d, i: (d, 0, 0, 0)),
                  st_spec],
        out_specs=(tok2, st_spec),
        out_shape=(jax.ShapeDtypeStruct((2, bsz, t_len, HD), BF16),
                   jax.ShapeDtypeStruct((bsz, 2, A_HEADS, A_DV, A_DK), F32)),
        scratch_shapes=[pltpu.VMEM((A_HEADS, A_DV, A_DK), F32)],
        compiler_params=_cp(("arbitrary", "arbitrary", "arbitrary")),
    )(q, k2, lf2, v, tri, masks, s0)


def _unit_tri_inverse_minus_identity(a_list, blocks_ref):
    b16 = lambda x: x.astype(BF16)
    eye = (lax.broadcasted_iota(jnp.int32, (TB, TB), 0) == lax.broadcasted_iota(jnp.int32, (TB, TB), 1))
    plus_eye = lambda e: b16(jnp.where(eye, 1.0, e))
    ps = [a * blocks_ref[0] for a in a_list]
    es = [-p for p in ps]
    width = 1
    while 2 * width < SUB:
        ps = [_dot(b16(p), b16(p)) for p in ps]
        es = [e + _dot(plus_eye(e), b16(p)) for e, p in zip(es, ps)]
        width *= 2
    for lv in range(len(LEVELS)):
        xs = [plus_eye(e) for e in es]
        ys = [_dot(x, b16(a * blocks_ref[lv + 1])) for x, a in zip(xs, a_list)]
        es = [e - _dot(b16(y), x) for e, y, x in zip(es, ys, xs)]
    return es


def _gdn_scan_kernel(nb, q_ref, k_ref, v_ref, gbc_ref, gbr_ref, linc_ref, tri_ref, blocks_ref, s0_ref,
                     o_ref, sfin_ref, st_ref):
    i = pl.program_id(2)
    nt = pl.num_programs(2)

    @pl.when(i == 0)
    def _():
        st_ref[...] = s0_ref[:, 0]

    linc = linc_ref[0]
    incl = tri_ref[0, 0] > 0.0
    strict = tri_ref[0, 1]
    gcols, gc_cols, gc_rows, g_tots = [], [], [], []
    for bb in range(nb):
        gcol = gbc_ref[bb, 0]
        grow = gbr_ref[bb, 0]
        c_hi, c_lo = _hi_lo(gcol)
        gc_col = _dot(linc, jnp.concatenate([c_hi, c_lo], axis=1))
        gc_cols.append(gc_col[:, :LANE] + gc_col[:, LANE:])
        r_hi, r_lo = _hi_lo(grow)
        gc_row = _dot_nt(jnp.concatenate([r_hi, r_lo], axis=0), linc)
        gc_rows.append(gc_row[:SUBLANE] + gc_row[SUBLANE:])
        g_tots.append(jnp.sum(gcol, axis=0, keepdims=True))
        gcols.append(gcol)

    chains = [(bb, hh) for bb in range(nb) for hh in range(B_HEADS)]
    idx = range(len(chains))
    sls = [slice(hh * B_DK, (hh + 1) * B_DK) for _, hh in chains]
    qs = [q_ref.at[bb] for bb, _ in chains]
    ks = [k_ref.at[bb] for bb, _ in chains]
    vs = [v_ref.at[bb] for bb, _ in chains]
    gccs = [gc_cols[bb][:, hh:hh + 1] for bb, hh in chains]
    tots = [g_tots[bb][:, hh:hh + 1] for bb, hh in chains]
    betas = [gcols[bb][:, B_HEADS + hh:B_HEADS + hh + 1] for bb, hh in chains]
    a_list, attns, kbetas = [], [], []
    for c, (bb, hh) in enumerate(chains):
        q = qs[c][:, sls[c]]
        k = ks[c][:, sls[c]]
        gcr = gc_rows[bb][hh:hh + 1, :]
        decay = jnp.where(incl, jnp.exp(jnp.where(incl, gccs[c] - gcr, 0.0)), 0.0)
        kbeta = k * betas[c]
        prod = _dot_nt(jnp.concatenate([kbeta, q], axis=0).astype(BF16), k.astype(BF16))
        a_list.append(strict * prod[:TB] * decay)
        attns.append(prod[TB:] * decay)
        kbetas.append(kbeta)
    es = _unit_tri_inverse_minus_identity(a_list, blocks_ref)
    egs = [jnp.exp(gccs[c]) for c in idx]
    sols = []
    for c in idx:
        rhs = jnp.concatenate([vs[c][:, sls[c]] * betas[c], kbetas[c] * egs[c]], axis=1)
        sols.append(rhs + _dot(es[c].astype(BF16), rhs.astype(BF16)))
    sts = [st_ref[bb, hh] for bb, hh in chains]
    ws_qs = [_dot_nt(jnp.concatenate([sols[c][:, B_DV:], qs[c][:, sls[c]] * egs[c]], axis=0).astype(BF16),
                     sts[c].astype(BF16)) for c in idx]
    v_news = [sols[c][:, :B_DV] - ws_qs[c][:TB] for c in idx]
    for c, (bb, hh) in enumerate(chains):
        o = ws_qs[c][TB:] + _dot(attns[c].astype(BF16), v_news[c].astype(BF16))
        o_ref[0, bb, :, sls[c]] = o.astype(o_ref.dtype)
    for c, (bb, hh) in enumerate(chains):
        kd = ks[c][:, sls[c]] * jnp.exp(tots[c] - gccs[c])
        st_ref[bb, hh] = sts[c] * jnp.exp(tots[c]) + _dot_tn(v_news[c].astype(BF16), kd.astype(BF16))

    @pl.when(i == nt - 1)
    def _():
        sfin_ref[:, 0] = st_ref[...]


def _gdn_scan(q, k, v, gbc, gbr, s0):
    bsz, t_len, _ = q.shape
    nt = t_len // TB
    linc, tri, blocks = _gdn_consts()
    nb = GDN_NB
    tok = pl.BlockSpec((nb, TB, HD), lambda b, d, i: (b, _time_block(d, i, nt), 0))
    tok2 = pl.BlockSpec((1, nb, TB, HD), lambda b, d, i: (d, b, _time_block(d, i, nt), 0))
    st_spec = pl.BlockSpec((nb, 1, B_HEADS, B_DV, B_DK), lambda b, d, i: (b, d, 0, 0, 0))
    return pl.pallas_call(
        functools.partial(_gdn_scan_kernel, nb),
        grid=(bsz // nb, 2, nt),
        in_specs=[tok, tok, tok,
                  pl.BlockSpec((nb, 1, TB, LANE), lambda b, d, i: (b, d, _time_block(d, i, nt), 0)),
                  pl.BlockSpec((nb, 1, SUBLANE, TB), lambda b, d, i: (b, d, 0, _time_block(d, i, nt))),
                  pl.BlockSpec((1, TB, TB), lambda b, d, i: (d, 0, 0)),
                  pl.BlockSpec((1, 2, TB, TB), lambda b, d, i: (d, 0, 0, 0)),
                  _const_spec(blocks.shape),
                  st_spec],
        out_specs=(tok2, st_spec),
        out_shape=(jax.ShapeDtypeStruct((2, bsz, t_len, HD), BF16),
                   jax.ShapeDtypeStruct((bsz, 2, B_HEADS, B_DV, B_DK), F32)),
        scratch_shapes=[pltpu.VMEM((nb, B_HEADS, B_DV, B_DK), F32)],
        compiler_params=_cp(("arbitrary", "arbitrary", "arbitrary")),
    )(q, k, v, gbc, gbr, linc, tri, blocks, s0)


def _head_norm_gate(o, g, gate, heads, width):
    parts = []
    for hh in range(heads):
        t = o[:, hh * width:(hh + 1) * width]
        ms = jnp.mean(t * t, axis=-1, keepdims=True)
        parts.append(t * lax.rsqrt(ms + EPS) * g)
    return jnp.concatenate(parts, axis=1) * gate


def _ab_out_kernel(x_ref, oaf_ref, oab_ref, obf_ref, obb_ref, ga_ref, zb_ref, mod_ref, ag_ref, bg_ref, w_ref,
                   o_ref):
    f32 = lambda ref, *idx: ref[idx].astype(F32)
    ya = _head_norm_gate(f32(oaf_ref, 0, 0) + f32(oab_ref, 0, 0), ag_ref[...], f32(ga_ref, 0), A_HEADS, A_DV)
    yb = _head_norm_gate(f32(obf_ref, 0, 0) + f32(obb_ref, 0, 0), bg_ref[...], f32(zb_ref, 0), B_HEADS, B_DV)
    y = _dot(jnp.concatenate([ya, yb], axis=1).astype(BF16), w_ref[...])
    o_ref[0] = x_ref[0] + mod_ref[0, 2:3, :] * y


def _ab_out(x, oa, ob, ga, zb, mod, per_batch, hgrn_g, gdn_g, w_out):
    bsz, t_len, _ = x.shape
    tm = min(AB_OUT_TM, t_len)
    tok = lambda width: pl.BlockSpec((1, tm, width), lambda b, i: (b, i, 0))
    dir_spec = lambda d: pl.BlockSpec((1, 1, tm, HD), lambda b, i: (d, b, i, 0))
    return pl.pallas_call(
        _ab_out_kernel,
        grid=(bsz, t_len // tm),
        in_specs=[tok(D_MODEL), dir_spec(0), dir_spec(1), dir_spec(0), dir_spec(1), tok(HD), tok(HD),
                  _mod_spec(per_batch), _const_spec((1, A_DV)), _const_spec((1, B_DV)),
                  _const_spec(w_out.shape)],
        out_specs=tok(D_MODEL),
        out_shape=jax.ShapeDtypeStruct(x.shape, F32),
        compiler_params=_cp(("arbitrary", "arbitrary")),
    )(x, oa, oa, ob, ob, ga, zb, mod, hgrn_g.reshape(1, A_DV), gdn_g.reshape(1, B_DV), w_out)


def _ffn_kernel(tm, halo, row_w, final, x_ref, xp_ref, xn_ref, mod_ref, ng_ref, wa_ref, wv_ref, cw_ref, cb_ref,
                wd_ref, fg_ref, o_ref, h_ref, a_ref, g_ref):
    i = pl.program_id(1)
    nt = pl.num_programs(1)
    sh = mod_ref[0, 3:4, :]
    sc = mod_ref[0, 4:5, :]
    g = ng_ref[...]
    x = x_ref[0]
    h_ref[halo:halo + tm, :] = _norm_mod(x, g, sc, sh).astype(BF16)
    if halo:
        h_ref[0:halo, :] = _norm_mod(xp_ref[0], g, sc, sh).astype(BF16)
        h_ref[halo + tm:, :] = _norm_mod(xn_ref[0], g, sc, sh).astype(BF16)
        keep_p = (i > 0).astype(F32)
        keep_n = (i < nt - 1).astype(F32)
    pad = SUBLANE
    zeros = jnp.zeros((pad, FT), F32)
    for slot in range(2):
        a_ref[slot, 0:pad, :] = zeros
        a_ref[slot, pad + tm + 2 * halo:, :] = zeros
    col = lax.broadcasted_iota(jnp.int32, (tm, FT), 0) % row_w
    has_left = col >= 1
    has_right = col <= row_w - 2
    dys = (-1, 0, 1) if halo else (0,)
    n_tiles = FFN_DIM // FT

    def up(f):
        a = _dot(h_ref[...], wa_ref[f])
        slot = f % 2
        if halo:
            a_ref[slot, pad:pad + halo, :] = a[0:halo] * keep_p
            a_ref[slot, pad + halo:pad + halo + tm, :] = a[halo:halo + tm]
            a_ref[slot, pad + halo + tm:pad + tm + 2 * halo, :] = a[halo + tm:] * keep_n
        else:
            a_ref[slot, pad:pad + tm, :] = a
        return _dot(h_ref[halo:halo + tm, :], wv_ref[f])

    val = up(0)
    down = None
    for f in range(n_tiles):
        nxt_val = up(f + 1) if f + 1 < n_tiles else None
        slot = f % 2
        cw = cw_ref[f]
        cols = []
        for dx in (-1, 0, 1):
            s = None
            for dy in dys:
                tap = cw[(dy + 1) * FFN_CONV + dx + 1:(dy + 1) * FFN_CONV + dx + 2, :]
                term = tap * a_ref[slot, pl.ds(pad + halo + dy * row_w + dx, tm), :]
                s = term if s is None else s + term
            cols.append(s)
        conv = cols[1] + jnp.where(has_left, cols[0], 0.0) + jnp.where(has_right, cols[2], 0.0)
        g_ref[:, f * FT:(f + 1) * FT] = (_silu(conv + cb_ref[f]) * val).astype(BF16)
        val = nxt_val
        if f + 1 in DOWN_PARTS:
            lo = ([0] + list(DOWN_PARTS))[DOWN_PARTS.index(f + 1)] * FT
            part = _dot(g_ref[:, lo:(f + 1) * FT], wd_ref[lo:(f + 1) * FT, :])
            down = part if down is None else down + part
    y = x + mod_ref[0, 5:6, :] * down
    if final:
        ms = jnp.mean(y * y, axis=-1, keepdims=True)
        y = y * lax.rsqrt(ms + EPS) * fg_ref[...]
    o_ref[0] = y


def _ffn(x, mod, per_batch, ng, w, latent, final_g=None):
    bsz, t_len, _ = x.shape
    tm = FFN_TM if latent else t_len
    halo = GRID_W if latent else 0
    row_w = GRID_W if latent else t_len
    hb = max(halo, SUBLANE)
    per = tm // hb
    last = t_len // hb - 1
    main = pl.BlockSpec((1, tm, D_MODEL), lambda b, i: (b, i, 0))
    prev = pl.BlockSpec((1, hb, D_MODEL), lambda b, i: (b, jnp.maximum(i * per - 1, 0), 0))
    nxt = pl.BlockSpec((1, hb, D_MODEL), lambda b, i: (b, jnp.minimum((i + 1) * per, last), 0))
    final = final_g is not None
    fg = (final_g if final else jnp.ones((D_MODEL,), F32)).reshape(1, D_MODEL)
    kern = functools.partial(_ffn_kernel, tm, halo, row_w, final)
    return pl.pallas_call(
        kern,
        grid=(bsz, t_len // tm),
        in_specs=[main, prev, nxt, _mod_spec(per_batch), _const_spec((1, D_MODEL)),
                  _const_spec(w['a'].shape), _const_spec(w['v'].shape), _const_spec(w['cw'].shape),
                  _const_spec(w['cb'].shape), _const_spec(w['d'].shape), _const_spec((1, D_MODEL))],
        out_specs=main,
        out_shape=jax.ShapeDtypeStruct(x.shape, F32),
        scratch_shapes=[pltpu.VMEM((tm + 2 * halo, D_MODEL), BF16),
                        pltpu.VMEM((2, tm + 2 * halo + 2 * SUBLANE, FT), F32),
                        pltpu.VMEM((tm, FFN_DIM), BF16)],
        compiler_params=_cp(("arbitrary", "arbitrary")),
    )(x, x, x, mod, ng.reshape(1, D_MODEL), w['a'], w['v'], w['cw'], w['cb'], w['d'], fg)


def _ffn_weights(w_up, conv_w, conv_b, w_down):
    nf = FFN_DIM // FT
    tiles = lambda m: m.reshape(D_MODEL, nf, FT).transpose(1, 0, 2).astype(BF16)
    return {'a': tiles(w_up[:, :FFN_DIM]), 'v': tiles(w_up[:, FFN_DIM:]),
            'cw': conv_w.reshape(FFN_CONV * FFN_CONV, nf, FT).transpose(1, 0, 2),
            'cb': conv_b.reshape(nf, 1, FT),
            'd': w_down.astype(BF16)}


SCAN_CHUNKS = SUBLANE
SCAN_LEN = TB // SCAN_CHUNKS
GDN_NB = 1
RGLRU_NB = 2


def _chunk_carries(a_tot, u_tot, carry, fwd):
    row = lax.broadcasted_iota(jnp.int32, a_tot.shape, 0)
    a, u = a_tot, u_tot
    s = 1
    while s < SCAN_CHUNKS:
        has_prev = (row >= s) if fwd else (row < SCAN_CHUNKS - s)
        shift = s if fwd else SCAN_CHUNKS - s
        a_s = jnp.where(has_prev, pltpu.roll(a, shift, axis=0), 1.0)
        u_s = jnp.where(has_prev, pltpu.roll(u, shift, axis=0), 0.0)
        a, u = a * a_s, a * u_s + u
        s *= 2
    after = a * carry + u
    if fwd:
        entering = jnp.where(row >= 1, pltpu.roll(after, 1, axis=0), carry)
        return entering, after[SCAN_CHUNKS - 1:SCAN_CHUNKS, :]
    entering = jnp.where(row < SCAN_CHUNKS - 1, pltpu.roll(after, SCAN_CHUNKS - 1, axis=0), carry)
    return entering, after[0:1, :]


class _RglruStream:
    def __init__(self, s, fwd, with_out, tb, nt, refs):
        self.s, self.fwd, self.with_out = s, fwd, with_out
        (x_ref, xp_ref, xn_ref, mod_ref, ng_ref, self.wx_ref, self.cw_ref, self.cb_ref, self.gw_ref, self.gb_ref,
         self.lam_ref, perm_ref, _), refs = refs[:13], refs[13:]
        if fwd and with_out:
            (self.hb_ref, self.wg_ref, self.wo_ref, self.unperm_ref, self.o_ref), refs = refs[:5], refs[5:]
        elif not fwd:
            (self.o_ref,), refs = refs[:1], refs[1:]
        _, self.ext_ref, self.a_ref, self.u_ref, self.hl_ref, self.p_ref, self.hs_ref, self.carry_ref = refs
        m = min(s, mod_ref.shape[0] - 1)
        self.mod_ref, self.m = mod_ref, m
        sh = mod_ref[m, 0:1, :]
        sc = mod_ref[m, 1:2, :]
        g = ng_ref[...]
        self.x = x_ref[s]
        self.h_all = jnp.concatenate(
            [_dot(perm_ref[...], _norm_mod(self.x, g, sc, sh).astype(BF16)), _norm_mod(xp_ref[s], g, sc, sh),
             _norm_mod(xn_ref[s], g, sc, sh)], axis=0).astype(BF16)
        self.keep_before = (tb > 0).astype(F32)
        self.keep_after = (tb < nt - 1).astype(F32)
        self.row = lax.broadcasted_iota(jnp.int32, (SCAN_CHUNKS, C_BLOCK), 0)
        self.steps = list(range(SCAN_LEN)) if fwd else list(reversed(range(SCAN_LEN)))
        self.sls = [slice(hh * C_BLOCK, (hh + 1) * C_BLOCK) for hh in range(C_HEADS)]

    def project(self, hh):
        gate_pre = _dot(self.h_all[:TB], self.wg_ref[:, self.sls[hh]]) if self.fwd and self.with_out else None
        return _dot(self.h_all, self.wx_ref[:, self.sls[hh]]), gate_pre

    def features(self, hh, proj):
        s, sl, row = self.s, self.sls[hh], self.row
        left = C_CONV // 2
        right = C_CONV - 1 - left
        ext = self.ext_ref.at[s, hh % 2]
        xb = proj[:TB]
        before = proj[TB:TB + CONV_HALO] * self.keep_before
        after = proj[TB + CONV_HALO:] * self.keep_after
        ext[left * SCAN_CHUNKS:left * SCAN_CHUNKS + TB, :] = xb
        for k in range(1, left + 1):
            grp = xb[TB - k * SCAN_CHUNKS:TB - (k - 1) * SCAN_CHUNKS]
            halo = before[CONV_HALO - k:CONV_HALO - k + 1, :]
            ext[(left - k) * SCAN_CHUNKS:(left - k + 1) * SCAN_CHUNKS, :] = jnp.where(
                row == 0, halo, pltpu.roll(grp, 1, axis=0))
        for k in range(1, right + 1):
            grp = xb[(k - 1) * SCAN_CHUNKS:k * SCAN_CHUNKS]
            halo = after[k - 1:k, :]
            lo = (left + k - 1) * SCAN_CHUNKS + TB
            ext[lo:lo + SCAN_CHUNKS, :] = jnp.where(
                row == SCAN_CHUNKS - 1, halo, pltpu.roll(grp, SCAN_CHUNKS - 1, axis=0))
        xh = self.cb_ref[:, sl]
        for k in range(C_CONV):
            xh = xh + self.cw_ref[k:k + 1, sl] * ext[k * SCAN_CHUNKS:k * SCAN_CHUNKS + TB, :]
        gates = _sigmoid(_dot(xh.astype(BF16), self.gw_ref[hh]) + self.gb_ref[hh])
        a = jnp.exp(gates[:, :C_BLOCK] * ((-C_GATE_C) * _softplus(-self.lam_ref[:, sl])))
        self.a_ref[s, :, sl] = a
        self.u_ref[s, :, sl] = jnp.sqrt(1.0 - a * a) * (gates[:, C_BLOCK:] * xh)

    def scan(self, hh):
        s, sl = self.s, self.sls[hh]
        h_loc = p_loc = None
        for j in self.steps:
            rows = slice(j * SCAN_CHUNKS, (j + 1) * SCAN_CHUNKS)
            a_j = self.a_ref[s, rows, sl]
            u_j = self.u_ref[s, rows, sl]
            h_loc, p_loc = (u_j, a_j) if h_loc is None else (a_j * h_loc + u_j, a_j * p_loc)
            self.hl_ref[s, rows, sl] = h_loc
            self.p_ref[s, rows, sl] = p_loc
        entering, leaving = _chunk_carries(p_loc, h_loc, self.carry_ref[s, :, sl], self.fwd)
        self.carry_ref[s, :, sl] = leaving
        if self.fwd and not self.with_out:
            return
        for j in self.steps:
            rows = slice(j * SCAN_CHUNKS, (j + 1) * SCAN_CHUNKS)
            h_j = self.hl_ref[s, rows, sl] + self.p_ref[s, rows, sl] * entering
            if self.fwd:
                self.hs_ref[s, rows, sl] = h_j + self.hb_ref[s, rows, sl]
            else:
                self.o_ref[s, rows, sl] = h_j

    def gated_out(self, hh, gate_pre):
        sl = self.sls[hh]
        z = _dot(self.unperm_ref[...], (_gelu_tanh(gate_pre) * self.hs_ref[self.s, :, sl]).astype(BF16))
        return _dot(z.astype(BF16), self.wo_ref[sl, :])

    def finish(self, y):
        self.o_ref[self.s] = self.x + self.mod_ref[self.m, 2:3, :] * y


def _rglru_pass_kernel(fwd, with_out, nb, *refs):
    h0_ref = refs[12]
    hfin_ref = refs[-8]
    carry_ref = refs[-1]
    i = pl.program_id(1)
    nt = pl.num_programs(1)
    tb = i if fwd else nt - 1 - i

    @pl.when(i == 0)
    def _():
        carry_ref[...] = h0_ref[...]

    streams = [_RglruStream(s, fwd, with_out, tb, nt, refs) for s in range(nb)]
    projs = [[st.project(0)] for st in streams]
    ys = [None] * nb
    for hh in range(C_HEADS):
        if hh + 1 < C_HEADS:
            for s, st in enumerate(streams):
                projs[s].append(st.project(hh + 1))
        for s, st in enumerate(streams):
            st.features(hh, projs[s][hh][0])
        if fwd and with_out and hh > 0:
            for s, st in enumerate(streams):
                part = st.gated_out(hh - 1, projs[s][hh - 1][1])
                ys[s] = part if ys[s] is None else ys[s] + part
        for st in streams:
            st.scan(hh)

    @pl.when(i == nt - 1)
    def _():
        hfin_ref[...] = carry_ref[...]

    if fwd and with_out:
        for s, st in enumerate(streams):
            st.finish(ys[s] + st.gated_out(C_HEADS - 1, projs[s][C_HEADS - 1][1]))


@functools.lru_cache(maxsize=None)
def _scan_order():
    t = np.arange(TB)
    perm = np.zeros((TB, TB), np.float32)
    perm[(t % SCAN_LEN) * SCAN_CHUNKS + t // SCAN_LEN, t] = 1.0
    return jnp.asarray(perm, BF16), jnp.asarray(perm.T, BF16)


def _rglru_pass(fwd, x, mod, per_batch, ng, wx, conv_w, conv_b, gate_w, gate_b, lam, h0, out_side=None):
    bsz, t_len, _ = x.shape
    nt = t_len // TB
    per = TB // CONV_HALO
    last = t_len // CONV_HALO - 1
    tbi = (lambda i: i) if fwd else (lambda i: nt - 1 - i)
    nb = RGLRU_NB
    main = pl.BlockSpec((nb, TB, D_MODEL), lambda b, i: (b, tbi(i), 0))
    prev = pl.BlockSpec((nb, CONV_HALO, D_MODEL), lambda b, i: (b, jnp.maximum(tbi(i) * per - 1, 0), 0))
    nxt = pl.BlockSpec((nb, CONV_HALO, D_MODEL), lambda b, i: (b, jnp.minimum((tbi(i) + 1) * per, last), 0))
    st_spec = pl.BlockSpec((nb, 1, C_WIDTH), lambda b, i: (b, 0, 0))
    perm, unperm = _scan_order()
    args = [x, x, x, mod, ng.reshape(1, D_MODEL), wx, conv_w, conv_b.reshape(1, C_WIDTH), gate_w, gate_b, lam, perm,
            h0]
    mod_spec = (pl.BlockSpec((nb, SUBLANE, D_MODEL), lambda b, i: (b, 0, 0)) if per_batch else _mod_spec(False))
    in_specs = [main, prev, nxt, mod_spec, _const_spec((1, D_MODEL)), _const_spec(wx.shape),
                _const_spec(conv_w.shape), _const_spec((1, C_WIDTH)), _const_spec(gate_w.shape),
                _const_spec(gate_b.shape), _const_spec(lam.shape), _const_spec(perm.shape), st_spec]
    st_shape = jax.ShapeDtypeStruct((bsz, 1, C_WIDTH), F32)
    tok_shape = jax.ShapeDtypeStruct((bsz, t_len, C_WIDTH), F32)
    with_out = out_side is not None
    if fwd and with_out:
        h_bwd, wg, wo = out_side
        args += [h_bwd, wg, wo, unperm]
        in_specs += [main, _const_spec(wg.shape), _const_spec(wo.shape), _const_spec(unperm.shape)]
    if fwd and not with_out:
        out_specs, out_shape = (st_spec,), (st_shape,)
    else:
        out_specs, out_shape = (main, st_spec), (tok_shape, st_shape)
    blk = lambda: pltpu.VMEM((nb, TB, C_WIDTH), F32)
    return pl.pallas_call(
        functools.partial(_rglru_pass_kernel, fwd, with_out, nb),
        grid=(bsz // nb, nt),
        in_specs=in_specs,
        out_specs=out_specs,
        out_shape=out_shape,
        scratch_shapes=[pltpu.VMEM((nb, 2, TB + (C_CONV - 1) * SCAN_CHUNKS, C_BLOCK), F32), blk(), blk(), blk(), blk(),
                        blk(), pltpu.VMEM((nb, 1, C_WIDTH), F32)],
        compiler_params=_cp(("arbitrary", "arbitrary")),
    )(*args)


def _mod_rows(mods_l, bsz):
    m = mods_l.reshape(mods_l.shape[0], 6, D_MODEL)
    m = jnp.pad(m, ((0, 0), (0, SUBLANE - 6), (0, 0)))
    return m[:bsz], m[bsz:bsz + 1]


def kernel(x, c, ctx, c_ctx, mod_w, mod_b, norm_mix_g, norm_ffn_g, final_norm_g, ab_w_in, hgrn_lb, hgrn_norm_g, gdn_conv_w, gdn_a_log, gdn_dt_bias, gdn_norm_g, ab_w_out, c_w_in, c_conv_w, c_conv_b, c_gate_w, c_gate_b, c_lambda, c_w_out, ffn_w_up, ffn_conv_w, ffn_conv_b, ffn_w_down):
    bsz = x.shape[0]
    rows = 2 * SUBLANE
    cc = jnp.concatenate([c, c_ctx[None, :], jnp.zeros((rows - bsz - 1, D_MODEL), F32)], axis=0)
    mods = _modulation(cc, mod_w, mod_b)
    xc = ctx
    for l in range(DEPTH):
        last = l == DEPTH - 1
        mod_x, mod_c = _mod_rows(mods[l], bsz)
        ffn_w = _ffn_weights(ffn_w_up[l], ffn_conv_w[l], ffn_conv_b[l], ffn_w_down[l])
        if l % 2 == 0:
            e = l // 2
            w_in = ab_w_in[e]
            o = 5 * HD
            w = {'a': w_in[:, :o].astype(BF16), 'qkv': w_in[:, o:o + QKV].astype(BF16),
                 'z': w_in[:, o + QKV:o + QKV + HD].astype(BF16),
                 'alpha': w_in[:, o + QKV + HD:o + QKV + HD + 2 * B_HEADS],
                 'beta': w_in[:, o + QKV + HD + 2 * B_HEADS:]}
            w_out = ab_w_out[e].astype(BF16)
            feats = []
            for seq, mod, per_batch in ((xc, mod_c, False), (x, mod_x, True)):
                feats.append(_ab_in(l, seq, mod, per_batch, norm_mix_g[l], w, hgrn_lb, gdn_conv_w[e],
                                    gdn_a_log[e], gdn_dt_bias[e]))
            outs = []
            sa = jnp.zeros((bsz, 2, A_HEADS, A_DV, A_DK), F32)
            sb = jnp.zeros((bsz, 2, B_HEADS, B_DV, B_DK), F32)
            for qa, ka, lfa, va, ga, qb, kb, vb, zb, gbc, gbr in feats:
                oa, sa = _hgrn_scan(qa, ka, lfa, va, sa)
                ob, sb = _gdn_scan(qb, kb, vb, gbc, gbr, sb)
                outs.append((oa, ob, ga, zb))
            oa, ob, ga, zb = outs[1]
            x = _ab_out(x, oa, ob, ga, zb, mod_x, True, hgrn_norm_g[e], gdn_norm_g[e], w_out)
            if not last:
                oa, ob, ga, zb = outs[0]
                xc = _ab_out(xc, oa, ob, ga, zb, mod_c, False, hgrn_norm_g[e], gdn_norm_g[e], w_out)
        else:
            o = l // 2
            wg = c_w_in[o][:, :C_WIDTH].astype(BF16)
            wx = c_w_in[o][:, C_WIDTH:].astype(BF16)
            wo = c_w_out[o].astype(BF16)
            gate_w = c_gate_w[o].astype(BF16)
            gate_b = c_gate_b[o].reshape(2, C_HEADS, 1, 2 * C_BLOCK)
            lam = c_lambda[o].reshape(2, 1, C_WIDTH)
            zero = jnp.zeros((bsz, 1, C_WIDTH), F32)

            def run(fwd, seq, mod, per_batch, h0, out_side=None):
                d = 0 if fwd else 1
                return _rglru_pass(fwd, seq, mod, per_batch, norm_mix_g[l], wx, c_conv_w[o], c_conv_b[o],
                                   gate_w[d], gate_b[d], lam[d], h0, out_side)

            hc_bwd, sb = run(False, xc, mod_c, False, zero)
            if last:
                (sf,) = run(True, xc, mod_c, False, zero)
            else:
                xc, sf = run(True, xc, mod_c, False, zero, (hc_bwd, wg, wo))
            h_bwd, _ = run(False, x, mod_x, True, sb)
            x, _ = run(True, x, mod_x, True, sf, (h_bwd, wg, wo))
        if not last:
            xc = _ffn(xc, mod_c, False, norm_ffn_g[l], ffn_w, latent=False)
        x = _ffn(x, mod_x, True, norm_ffn_g[l], ffn_w, latent=True, final_g=final_norm_g if last else None)
    return x
```
